```python
import jax, jax.numpy as jnp
from jax import lax
import numpy as np

D_MODEL = 1024
BATCH = 4
SEQ = 4096
DEPTH = 1

PLE_DIM = 256
EPS = 1e-6
N_HEADS = 8
QK_NOPE = 64
QK_ROPE = 32
QK_HEAD = QK_NOPE + QK_ROPE
V_HEAD = 64
Q_LORA = 256
KV_LORA = 128
ROPE_THETA = 10000.0
Q_BLOCK = 128
ATTN_WIDTH = N_HEADS * V_HEAD
POOL_WINDOWS = (2, 4, 8, 16)
POOL_GROUPS = 4
POOL_WIDTH = 512
POOL_GROUP_DIM = POOL_WIDTH // POOL_GROUPS
N_EXPERTS = 32
TOP_K = 4
D_EXPERT = 1024
SWIGLU_ALPHA = 1.702
SWIGLU_LIMIT = 7.0
EXPERT_BLOCK = 128
IN_SPLITS = (Q_LORA, Q_LORA + KV_LORA, Q_LORA + KV_LORA + QK_ROPE,
             Q_LORA + KV_LORA + QK_ROPE + POOL_WIDTH)
IN_PROJ_WIDTH = Q_LORA + KV_LORA + QK_ROPE + POOL_WIDTH + 2 * D_MODEL

kernel_name = "hybrid_mla_pool_moe_ple_block"


def rmsnorm(x, g):
    xf = x.astype(jnp.float32)
    y = xf * lax.rsqrt(jnp.mean(xf * xf, axis=-1, keepdims=True) + EPS)
    return (y * g.astype(jnp.float32)).astype(x.dtype)


def rope(x, cos, sin):
    x1, x2 = jnp.split(x.astype(jnp.float32), 2, axis=-1)
    return jnp.concatenate([x1 * cos - x2 * sin, x2 * cos + x1 * sin], axis=-1).astype(x.dtype)


def mla_branch(q_lat, kv_lat, k_rope_in, positions, g_q_lat, w_q_b, g_kv_lat, w_kv_b,
               g_q_head, g_k_nope, g_k_rope):
    B, S, _ = q_lat.shape
    q = (rmsnorm(q_lat, g_q_lat) @ w_q_b).reshape(B, S, N_HEADS, QK_HEAD)
    q = rmsnorm(q, g_q_head)
    kv = (rmsnorm(kv_lat, g_kv_lat) @ w_kv_b).reshape(B, S, N_HEADS, QK_NOPE + V_HEAD)
    k_nope, v = kv[..., :QK_NOPE], kv[..., QK_NOPE:]
    k_nope = rmsnorm(k_nope, g_k_nope)
    k_rope = rmsnorm(k_rope_in, g_k_rope)

    inv_freq = ROPE_THETA ** (-jnp.arange(0, QK_ROPE, 2, dtype=jnp.float32) / QK_ROPE)
    ang = positions.astype(jnp.float32)[..., None] * inv_freq
    cos, sin = jnp.cos(ang), jnp.sin(ang)
    q = jnp.concatenate([q[..., :QK_NOPE], rope(q[..., QK_NOPE:], cos[:, :, None], sin[:, :, None])], axis=-1)
    k_rope = rope(k_rope, cos, sin)
    k = jnp.concatenate([k_nope, jnp.broadcast_to(k_rope[:, :, None, :], (B, S, N_HEADS, QK_ROPE))], axis=-1)

    n_blocks = S // Q_BLOCK
    qb = q.reshape(B, n_blocks, Q_BLOCK, N_HEADS, QK_HEAD).transpose(1, 0, 3, 2, 4)
    kt = k.transpose(0, 2, 1, 3)
    vt = v.transpose(0, 2, 1, 3)
    scale = 1.0 / np.sqrt(QK_HEAD)
    key_idx = jnp.arange(S)

    def attend(args):
        q_blk, blk = args
        s = jnp.einsum('bhqd,bhkd->bhqk', q_blk, kt).astype(jnp.float32) * scale
        q_idx = blk * Q_BLOCK + jnp.arange(Q_BLOCK)
        mask = q_idx[:, None] >= key_idx[None, :]
        s = jnp.where(mask, s, jnp.float32(-1e30))
        pr = jax.nn.softmax(s, axis=-1).astype(vt.dtype)
        return jnp.einsum('bhqk,bhkd->bhqd', pr, vt)

    out = lax.map(attend, (qb, jnp.arange(n_blocks)))
    return out.transpose(1, 0, 3, 2, 4).reshape(B, S, ATTN_WIDTH)


def pool_branch(u, w_pool, pool_scale):
    B, S, _ = u.shape
    uf = u.astype(jnp.float32).reshape(B, S, POOL_GROUPS, POOL_GROUP_DIM)
    c = jnp.concatenate([jnp.zeros((B, 1, POOL_GROUPS, POOL_GROUP_DIM), jnp.float32),
                         jnp.cumsum(uf, axis=1)], axis=1)
    t = jnp.arange(S)
    means = []
    for g, w in enumerate(POOL_WINDOWS):
        lo = jnp.maximum(t + 1 - w, 0)
        window_sum = c[:, 1:, g] - c[:, lo, g]
        count = (t + 1 - lo).astype(jnp.float32)[:, None]
        means.append(window_sum / count)
    pooled = (jnp.stack(means, axis=2) - uf).astype(u.dtype)
    mixed = jnp.einsum('bsgc,gcd->bsgd', pooled, w_pool)
    return mixed.reshape(B, S, POOL_WIDTH) * pool_scale


def moe(h, w_router, b_router, w_gate_up, b_gate_up, w_down, b_down):
    B, S, D = h.shape
    T = B * S
    A = T * TOP_K
    hf = h.reshape(T, D)
    logits = (hf @ w_router).astype(jnp.float32) + b_router.astype(jnp.float32)
    top_val, top_idx = lax.top_k(logits, TOP_K)
    gates = jax.nn.softmax(top_val, axis=-1)

    expert_flat = top_idx.reshape(A)
    token_flat = jnp.repeat(jnp.arange(T, dtype=jnp.int32), TOP_K)
    gate_flat = gates.reshape(A)
    order = jnp.argsort(expert_flat)
    e_sorted = expert_flat[order]
    counts = jax.ops.segment_sum(jnp.ones((A,), jnp.int32), expert_flat, num_segments=N_EXPERTS)
    padded = (counts + EXPERT_BLOCK - 1) // EXPERT_BLOCK * EXPERT_BLOCK
    start = jnp.cumsum(counts) - counts
    pad_end = jnp.cumsum(padded)
    pad_start = pad_end - padded
    dest = pad_start[e_sorted] + jnp.arange(A, dtype=jnp.int32) - start[e_sorted]

    n_blocks = (A + EXPERT_BLOCK - 1) // EXPERT_BLOCK + N_EXPERTS
    M = n_blocks * EXPERT_BLOCK
    tok_buf = jnp.full((M,), T, jnp.int32).at[dest].set(token_flat[order])
    gate_buf = jnp.zeros((M,), jnp.float32).at[dest].set(gate_flat[order])
    block_expert = jnp.minimum(
        jnp.searchsorted(pad_end, jnp.arange(n_blocks, dtype=jnp.int32) * EXPERT_BLOCK, side='right'),
        N_EXPERTS - 1)
    h_pad = jnp.concatenate([hf, jnp.zeros((1, D), hf.dtype)], axis=0)
    x_blocks = h_pad[tok_buf].reshape(n_blocks, EXPERT_BLOCK, D)

    def expert_block(args):
        xb, e = args
        gu = xb @ w_gate_up[e] + b_gate_up[e]
        gate, up = gu[:, :D_EXPERT], gu[:, D_EXPERT:]
        gate = jnp.minimum(gate, SWIGLU_LIMIT)
        up = jnp.clip(up, -SWIGLU_LIMIT, SWIGLU_LIMIT)
        act = (up + 1.0) * gate * jax.nn.sigmoid(SWIGLU_ALPHA * gate)
        return act @ w_down[e] + b_down[e]

    y = lax.map(expert_block, (x_blocks, block_expert)).reshape(M, D)
    y = y * gate_buf[:, None].astype(y.dtype)
    out = jnp.zeros((T + 1, D), y.dtype).at[tok_buf].add(y)[:T]
    return out.reshape(B, S, D)


def setup_inputs(seed: int = 0) -> dict:
    key = jax.random.key(seed)
    ks = jax.random.split(key, 32)

    def dense(k, shape, fan_in):
        return jax.random.normal(k, shape, jnp.float32) * (fan_in ** -0.5)

    def gain(k, shape):
        return 1.0 + 0.05 * jax.random.normal(k, shape, jnp.float32)

    def bias(k, shape):
        return 0.01 * jax.random.normal(k, shape, jnp.float32)

    L = DEPTH
    x = jax.random.normal(ks[0], (BATCH, SEQ, D_MODEL), jnp.float32)
    p = jax.random.normal(ks[1], (DEPTH, BATCH, SEQ, PLE_DIM), jnp.float32)
    offset = jax.random.randint(ks[2], (BATCH, 1), 0, 1024, dtype=jnp.int32)
    positions = offset + jnp.arange(SEQ, dtype=jnp.int32)[None, :]
    return {
        "x": x,
        "p": p,
        "positions": positions,
        "g_mix": gain(ks[3], (L, D_MODEL)),
        "w_in": dense(ks[4], (L, D_MODEL, IN_PROJ_WIDTH), D_MODEL),
        "g_q_lat": gain(ks[5], (L, Q_LORA)),
        "w_q_b": dense(ks[6], (L, Q_LORA, N_HEADS * QK_HEAD), Q_LORA),
        "g_kv_lat": gain(ks[7], (L, KV_LORA)),
        "w_kv_b": dense(ks[8], (L, KV_LORA, N_HEADS * (QK_NOPE + V_HEAD)), KV_LORA),
        "g_q_head": gain(ks[9], (L, QK_HEAD)),
        "g_k_nope": gain(ks[10], (L, QK_NOPE)),
        "g_k_rope": gain(ks[11], (L, QK_ROPE)),
        "w_pool": dense(ks[12], (L, POOL_GROUPS, POOL_GROUP_DIM, POOL_GROUP_DIM), POOL_GROUP_DIM),
        "pool_scale": gain(ks[13], (L, POOL_WIDTH)),
        "w_attn_out": dense(ks[14], (L, ATTN_WIDTH, D_MODEL), ATTN_WIDTH),
        "w_pool_out": dense(ks[15], (L, POOL_WIDTH, D_MODEL), POOL_WIDTH),
        "w_o": dense(ks[16], (L, D_MODEL, D_MODEL), D_MODEL),
        "g_ffn": gain(ks[17], (L, D_MODEL)),
        "w_router": dense(ks[18], (L, D_MODEL, N_EXPERTS), D_MODEL),
        "b_router": bias(ks[19], (L, N_EXPERTS)),
        "w_gate_up": dense(ks[20], (L, N_EXPERTS, D_MODEL, 2 * D_EXPERT), D_MODEL),
        "b_gate_up": bias(ks[21], (L, N_EXPERTS, 2 * D_EXPERT)),
        "w_down": dense(ks[22], (L, N_EXPERTS, D_EXPERT, D_MODEL), D_EXPERT),
        "b_down": bias(ks[23], (L, N_EXPERTS, D_MODEL)),
        "g_ple": gain(ks[24], (L, D_MODEL)),
        "w_ple_gate": dense(ks[25], (L, D_MODEL, D_MODEL), D_MODEL),
        "w_ple": dense(ks[26], (L, PLE_DIM, D_MODEL), PLE_DIM),
    }


def reference(x, p, positions, g_mix, w_in, g_q_lat, w_q_b, g_kv_lat, w_kv_b, g_q_head,
              g_k_nope, g_k_rope, w_pool, pool_scale, w_attn_out, w_pool_out, w_o, g_ffn,
              w_router, b_router, w_gate_up, b_gate_up, w_down, b_down, g_ple, w_ple_gate, w_ple):
    for i in range(DEPTH):
        h = rmsnorm(x, g_mix[i])
        proj = h @ w_in[i]
        q_lat, kv_lat, k_rope_in, pool_in, gate_logits = jnp.split(proj, IN_SPLITS, axis=-1)
        y_attn = mla_branch(q_lat, kv_lat, k_rope_in, positions, g_q_lat[i], w_q_b[i],
                            g_kv_lat[i], w_kv_b[i], g_q_head[i], g_k_nope[i], g_k_rope[i]) @ w_attn_out[i]
        y_pool = pool_branch(pool_in, w_pool[i], pool_scale[i]) @ w_pool_out[i]
        gate_a, gate_p = jnp.split(jax.nn.sigmoid(gate_logits), 2, axis=-1)
        x = x + (gate_a * y_attn + gate_p * y_pool) @ w_o[i]
        x = x + moe(rmsnorm(x, g_ffn[i]), w_router[i], b_router[i], w_gate_up[i], b_gate_up[i],
                    w_down[i], b_down[i])
        ple_gate = jax.nn.sigmoid(rmsnorm(x, g_ple[i]) @ w_ple_gate[i])
        x = x + ple_gate * (p[i] @ w_ple[i])
    return x
```

```python
import functools
import math

import jax
import jax.numpy as jnp
import numpy as np
from jax import lax
from jax.experimental import pallas as pl
from jax.experimental.pallas import tpu as pltpu

D_MODEL = 1024
N_HEADS = 8
QK_NOPE = 64
QK_ROPE = 32
QK_HEAD = QK_NOPE + QK_ROPE
V_HEAD = 64
Q_LORA = 256
KV_LORA = 128
ROPE_THETA = 10000.0
EPS = 1e-6
POOL_WINDOWS = (2, 4, 8, 16)
POOL_GROUPS = 4
POOL_WIDTH = 512
POOL_GROUP_DIM = POOL_WIDTH // POOL_GROUPS
POOL_HALO = 16
N_EXPERTS = 32
TOP_K = 4
D_EXPERT = 1024
SWIGLU_ALPHA = 1.702
SWIGLU_LIMIT = 7.0
PLE_DIM = 256

LANES = 128
HEAD_SLAB = LANES
VMEM_LIMIT_BYTES = 56 * 1024 * 1024

MXU_DTYPE = jnp.bfloat16
NEG_BIG = -1e30
Q_SCALE = math.log2(math.e) / math.sqrt(QK_HEAD)

TOKEN_TILE = 512
ATTN_TILE = 512
EXPERT_ROWS = 512


def _dot(a, b):
    return jnp.dot(a, b, preferred_element_type=jnp.float32)


def _rms_scale(v, width):
    return lax.rsqrt(jnp.sum(v * v, axis=-1, keepdims=True) * (1.0 / width) + EPS)


_C_QLAT = 0
_C_KVLAT = _C_QLAT + Q_LORA
_C_KR = _C_KVLAT + KV_LORA
_C_KRR = _C_KR + HEAD_SLAB
_C_POOL = _C_KRR + HEAD_SLAB
_C_GA = _C_POOL + POOL_WIDTH
_C_GP = _C_GA + D_MODEL
_C_END = _C_GP + D_MODEL


def _premix_kernel(tiles_per_seq, x_ref, pos_ref, gmix_ref, w1_ref, gql_ref, wqs_ref, wqr_ref,
                   gkvl_ref, wks_ref, wv_ref, gqs_ref, gqr_ref, gks_ref, gkrs_ref, gkrr_ref,
                   invf_ref, wpool_ref, pscale_ref, wpo_ref,
                   q_ref, kt_ref, v_ref, ga_ref, gp_ref, halo_ref):
    tm = x_ref.shape[0]
    si = pl.program_id(0) % tiles_per_seq

    xv = x_ref[...]
    h = xv * _rms_scale(xv, D_MODEL) * gmix_ref[...]
    proj = _dot(h.astype(MXU_DTYPE), w1_ref[...])

    ang = pos_ref[...] * invf_ref[...]
    cos_t = jnp.cos(ang)
    sin_t = jnp.sin(ang)
    lane = lax.broadcasted_iota(jnp.int32, (tm, LANES), 1)
    cos_q = jnp.where(lane < QK_NOPE, 1.0, cos_t)

    q_lat = proj[:, _C_QLAT:_C_QLAT + Q_LORA]
    qln = (q_lat * _rms_scale(q_lat, Q_LORA) * gql_ref[...]).astype(MXU_DTYPE)
    qs = _dot(qln, wqs_ref[...])
    qr = _dot(qln, wqr_ref[...])
    q_cos = gqs_ref[...] * cos_q
    q_sin = gqr_ref[...] * sin_t
    for hd in range(N_HEADS):
        s = qs[:, hd * HEAD_SLAB:(hd + 1) * HEAD_SLAB]
        r = qr[:, hd * HEAD_SLAB:(hd + 1) * HEAD_SLAB]
        scale = _rms_scale(s, QK_HEAD) * Q_SCALE
        q_ref[0, hd] = ((s * q_cos + r * q_sin) * scale).astype(q_ref.dtype)

    kv_lat = proj[:, _C_KVLAT:_C_KVLAT + KV_LORA]
    kvn = (kv_lat * _rms_scale(kv_lat, KV_LORA) * gkvl_ref[...]).astype(MXU_DTYPE)
    ks = _dot(kvn, wks_ref[...])
    v_ref[0] = _dot(kvn, wv_ref[...]).astype(v_ref.dtype)
    krs = proj[:, _C_KR:_C_KR + HEAD_SLAB]
    krr = proj[:, _C_KRR:_C_KRR + HEAD_SLAB]
    k_rot = (krs * (gkrs_ref[...] * cos_t) + krr * (gkrr_ref[...] * sin_t)) * _rms_scale(krs, QK_ROPE)
    for hd in range(N_HEADS):
        s = ks[:, hd * HEAD_SLAB:(hd + 1) * HEAD_SLAB]
        kh = s * _rms_scale(s, QK_NOPE) * gks_ref[...] + k_rot
        kt_ref[0, hd] = kh.T.astype(kt_ref.dtype)

    u = proj[:, _C_POOL:_C_POOL + POOL_WIDTH]

    @pl.when(si == 0)
    def _():
        halo_ref[...] = jnp.zeros_like(halo_ref)

    ext = jnp.concatenate([halo_ref[...], u], axis=0)
    halo_ref[...] = u[tm - POOL_HALO:, :]
    t_seq = si * tm + lax.broadcasted_iota(jnp.int32, (tm, 1), 0)
    mixed = []
    for g, w in enumerate(POOL_WINDOWS):
        acc = ext[:, g * POOL_GROUP_DIM:(g + 1) * POOL_GROUP_DIM]
        shift = 1
        while shift < w:
            acc = acc + pltpu.roll(acc, shift, axis=0)
            shift *= 2
        win = acc[POOL_HALO:, :]
        count = jnp.minimum(t_seq + 1, w).astype(jnp.float32)
        pooled = win / count - u[:, g * POOL_GROUP_DIM:(g + 1) * POOL_GROUP_DIM]
        mixed.append(_dot(pooled.astype(MXU_DTYPE), wpool_ref[g]))
    mixed = jnp.concatenate(mixed, axis=1) * pscale_ref[...]
    y_pool = _dot(mixed.astype(MXU_DTYPE), wpo_ref[...])

    ga_ref[...] = jax.nn.sigmoid(proj[:, _C_GA:_C_GA + D_MODEL]).astype(ga_ref.dtype)
    gp_ref[...] = (jax.nn.sigmoid(proj[:, _C_GP:_C_GP + D_MODEL]) * y_pool).astype(gp_ref.dtype)


def _slab_cols(width_per_head):
    return (np.arange(N_HEADS)[:, None] * HEAD_SLAB + np.arange(width_per_head)[None, :]).reshape(-1)


def _premix_weights(g_mix, w_in, g_q_lat, w_q_b, g_kv_lat, w_kv_b, g_q_head, g_k_nope, g_k_rope,
                    w_pool, pool_scale, w_pool_out):
    f32 = jnp.float32
    half = QK_ROPE // 2
    w_q_lat = w_in[:, :Q_LORA]
    w_kv_lat = w_in[:, Q_LORA:Q_LORA + KV_LORA]
    w_kr = w_in[:, Q_LORA + KV_LORA:Q_LORA + KV_LORA + QK_ROPE]
    off = Q_LORA + KV_LORA + QK_ROPE
    w_pool_in = w_in[:, off:off + POOL_WIDTH]
    w_ga = w_in[:, off + POOL_WIDTH:off + POOL_WIDTH + D_MODEL]
    w_gp = w_in[:, off + POOL_WIDTH + D_MODEL:]

    kr_s = jnp.zeros((D_MODEL, HEAD_SLAB), f32).at[:, QK_NOPE:QK_NOPE + QK_ROPE].set(w_kr)
    kr_r = jnp.zeros((D_MODEL, HEAD_SLAB), f32)
    kr_r = kr_r.at[:, QK_NOPE:QK_NOPE + half].set(-w_kr[:, half:])
    kr_r = kr_r.at[:, QK_NOPE + half:QK_NOPE + QK_ROPE].set(w_kr[:, :half])
    w1 = jnp.concatenate([w_q_lat, w_kv_lat, kr_s, kr_r, w_pool_in, w_ga, w_gp], axis=1)

    wq = w_q_b.reshape(Q_LORA, N_HEADS, QK_HEAD)
    wq_s = jnp.zeros((Q_LORA, N_HEADS, HEAD_SLAB), f32).at[:, :, :QK_HEAD].set(wq)
    wq_r = jnp.zeros((Q_LORA, N_HEADS, HEAD_SLAB), f32)
    wq_r = wq_r.at[:, :, QK_NOPE:QK_NOPE + half].set(-wq[:, :, QK_NOPE + half:])
    wq_r = wq_r.at[:, :, QK_NOPE + half:QK_HEAD].set(wq[:, :, QK_NOPE:QK_NOPE + half])
    wkv = w_kv_b.reshape(KV_LORA, N_HEADS, QK_NOPE + V_HEAD)
    wk_s = jnp.zeros((KV_LORA, N_HEADS, HEAD_SLAB), f32).at[:, :, :QK_NOPE].set(wkv[:, :, :QK_NOPE])
    wv = wkv[:, :, QK_NOPE:].reshape(KV_LORA, N_HEADS * V_HEAD)

    def slab(vals, start):
        return jnp.zeros((1, HEAD_SLAB), f32).at[0, start:start + vals.shape[0]].set(vals)

    gq_s = slab(g_q_head, 0)
    gq_r = slab(jnp.concatenate([g_q_head[QK_NOPE + half:], g_q_head[QK_NOPE:QK_NOPE + half]]), QK_NOPE)
    gk_s = slab(g_k_nope, 0)
    gkr_s = slab(g_k_rope, QK_NOPE)
    gkr_r = slab(jnp.concatenate([g_k_rope[half:], g_k_rope[:half]]), QK_NOPE)
    inv_freq = ROPE_THETA ** (-jnp.arange(0, QK_ROPE, 2, dtype=f32) / QK_ROPE)
    invf = jnp.tile(inv_freq, LANES // half)[None, :]
    bf = MXU_DTYPE
    return (g_mix[None, :], w1.astype(bf), g_q_lat[None, :],
            wq_s.reshape(Q_LORA, -1).astype(bf), wq_r.reshape(Q_LORA, -1).astype(bf),
            g_kv_lat[None, :], wk_s.reshape(KV_LORA, -1).astype(bf), wv.astype(bf),
            gq_s, gq_r, gk_s, gkr_s, gkr_r, invf, w_pool.astype(bf), pool_scale[None, :],
            w_pool_out.astype(bf))


def _full_spec(arr):
    nd = arr.ndim
    return pl.BlockSpec(arr.shape, lambda i, _nd=nd: (0,) * _nd)


def _premix(x2d, pos2d, weights, batch, seq):
    t = x2d.shape[0]
    tm = TOKEN_TILE
    tiles_per_seq = seq // tm
    in_specs = [pl.BlockSpec((tm, D_MODEL), lambda i: (i, 0)),
                pl.BlockSpec((tm, 1), lambda i: (i, 0))] + [_full_spec(w) for w in weights]
    out_shape = (
        jax.ShapeDtypeStruct((batch, N_HEADS, seq, HEAD_SLAB), MXU_DTYPE),
        jax.ShapeDtypeStruct((batch, N_HEADS, HEAD_SLAB, seq), MXU_DTYPE),
        jax.ShapeDtypeStruct((batch, seq, N_HEADS * V_HEAD), MXU_DTYPE),
        jax.ShapeDtypeStruct((t, D_MODEL), MXU_DTYPE),
        jax.ShapeDtypeStruct((t, D_MODEL), MXU_DTYPE),
    )
    out_specs = (
        pl.BlockSpec((1, N_HEADS, tm, HEAD_SLAB), lambda i: (i // tiles_per_seq, 0, i % tiles_per_seq, 0)),
        pl.BlockSpec((1, N_HEADS, HEAD_SLAB, tm), lambda i: (i // tiles_per_seq, 0, 0, i % tiles_per_seq)),
        pl.BlockSpec((1, tm, N_HEADS * V_HEAD), lambda i: (i // tiles_per_seq, i % tiles_per_seq, 0)),
        pl.BlockSpec((tm, D_MODEL), lambda i: (i, 0)),
        pl.BlockSpec((tm, D_MODEL), lambda i: (i, 0)),
    )
    return pl.pallas_call(
        functools.partial(_premix_kernel, tiles_per_seq),
        grid=(t // tm,),
        in_specs=in_specs,
        out_specs=out_specs,
        out_shape=out_shape,
        scratch_shapes=[pltpu.VMEM((POOL_HALO, POOL_WIDTH), jnp.float32)],
        compiler_params=pltpu.CompilerParams(
            dimension_semantics=("arbitrary",), vmem_limit_bytes=VMEM_LIMIT_BYTES),
        name="premix",
    )(x2d, pos2d, *weights)


def _attention_kernel(q_ref, kt_ref, v_ref, o_ref, va_ref, vb_ref):
    tq = q_ref.shape[2]
    tk = tq
    qi = pl.program_id(2)

    @pl.when(qi == 0)
    def _():
        v = v_ref[0]
        lane = lax.broadcasted_iota(jnp.int32, v.shape, 1)
        zero = jnp.zeros_like(v)
        va_ref[...] = jnp.where(lane < V_HEAD, v, zero)
        vb_ref[...] = jnp.where(lane < V_HEAD, zero, v)

    row = lax.broadcasted_iota(jnp.int32, (tq, tk), 0)
    col = lax.broadcasted_iota(jnp.int32, (tq, tk), 1)
    causal = row >= col

    def one_head(hh, vsel_ref):
        q = q_ref[0, hh]

        def step(j, carry, masked):
            m, l, acc = carry
            start = pl.multiple_of(j * tk, tk)
            s = _dot(q, kt_ref[0, hh, :, pl.ds(start, tk)])
            if masked:
                s = jnp.where(causal, s, NEG_BIG)
            m_new = jnp.maximum(m, jnp.max(s, axis=-1, keepdims=True))
            p = jnp.exp2(s - m_new)
            alpha = jnp.exp2(m - m_new)
            l = alpha * l + jnp.sum(p, axis=-1, keepdims=True)
            acc = alpha * acc + _dot(p.astype(MXU_DTYPE), vsel_ref[pl.ds(start, tk), :])
            return m_new, l, acc

        init = (jnp.full((tq, 1), NEG_BIG, jnp.float32), jnp.zeros((tq, 1), jnp.float32),
                jnp.zeros((tq, HEAD_SLAB), jnp.float32))
        carry = lax.fori_loop(0, qi, lambda j, c: step(j, c, False), init)
        _, l, acc = step(qi, carry, True)
        return acc / l

    o_ref[0] = (one_head(0, va_ref) + one_head(1, vb_ref)).astype(o_ref.dtype)


def _attention(q, kt, v):
    batch, _, seq, _ = q.shape
    tq = ATTN_TILE
    pairs = N_HEADS // 2
    return pl.pallas_call(
        _attention_kernel,
        grid=(batch, pairs, seq // tq),
        in_specs=[
            pl.BlockSpec((1, 2, tq, HEAD_SLAB), lambda b, p, i: (b, p, i, 0)),
            pl.BlockSpec((1, 2, HEAD_SLAB, seq), lambda b, p, i: (b, p, 0, 0)),
            pl.BlockSpec((1, seq, 2 * V_HEAD), lambda b, p, i: (b, 0, p)),
        ],
        out_specs=pl.BlockSpec((1, tq, 2 * V_HEAD), lambda b, p, i: (b, i, p)),
        out_shape=jax.ShapeDtypeStruct((batch, seq, N_HEADS * V_HEAD), MXU_DTYPE),
        scratch_shapes=[pltpu.VMEM((seq, 2 * V_HEAD), MXU_DTYPE), pltpu.VMEM((seq, 2 * V_HEAD), MXU_DTYPE)],
        compiler_params=pltpu.CompilerParams(
            dimension_semantics=("arbitrary", "arbitrary", "arbitrary"), vmem_limit_bytes=VMEM_LIMIT_BYTES),
        name="attention",
    )(q, kt, v)


def _lane_pack(cols, lane):
    out = jnp.zeros(lane.shape, cols[0].dtype)
    for k, c in enumerate(cols):
        out = jnp.where(lane == k, c, out)
    return out


def _postmix_kernel(x_ref, at_ref, ga_ref, gp_ref, wao_ref, wo_ref, gffn_ref, wrh_ref, wrl_ref, br_ref,
                    x1_ref, h2_ref, idx_ref, rank_ref, gate_ref, cnt_ref, carry_ref):
    tm = x_ref.shape[0]

    @pl.when(pl.program_id(0) == 0)
    def _():
        carry_ref[...] = jnp.zeros_like(carry_ref)

    y_attn = _dot(at_ref[...], wao_ref[...])
    merged = ga_ref[...].astype(jnp.float32) * y_attn + gp_ref[...].astype(jnp.float32)
    x1 = x_ref[...] + _dot(merged.astype(MXU_DTYPE), wo_ref[...])
    x1_ref[...] = x1
    h2 = x1 * _rms_scale(x1, D_MODEL) * gffn_ref[...]
    h2_ref[...] = h2

    h_hi = h2.astype(MXU_DTYPE)
    h_lo = (h2 - h_hi.astype(jnp.float32)).astype(MXU_DTYPE)
    logits = _dot(h_hi, wrh_ref[...]) + (_dot(h_lo, wrh_ref[...]) + _dot(h_hi, wrl_ref[...])) + br_ref[...]

    lane = lax.broadcasted_iota(jnp.int32, (tm, LANES), 1)
    work = logits
    vals, idxs = [], []
    for _ in range(TOP_K):
        mx = jnp.max(work, axis=-1, keepdims=True)
        ix = jnp.min(jnp.where(work == mx, lane, LANES), axis=-1, keepdims=True)
        vals.append(mx)
        idxs.append(ix)
        work = jnp.where(lane == ix, -jnp.inf, work)
    exps = [jnp.exp(v - vals[0]) for v in vals]
    denom = exps[0] + exps[1] + exps[2] + exps[3]
    gates = [e / denom for e in exps]

    onehot = jnp.zeros((tm, LANES), jnp.float32)
    for ix in idxs:
        onehot = onehot + (lane == ix).astype(jnp.float32)
    r_i = lax.broadcasted_iota(jnp.int32, (tm, tm), 0)
    c_i = lax.broadcasted_iota(jnp.int32, (tm, tm), 1)
    lower = (c_i < r_i).astype(MXU_DTYPE)
    before = carry_ref[...] + _dot(lower, onehot.astype(MXU_DTYPE))
    ranks = [jnp.sum(jnp.where(lane == ix, before, 0.0), axis=-1, keepdims=True).astype(jnp.int32)
             for ix in idxs]
    carry_ref[...] = carry_ref[...] + jnp.sum(onehot, axis=0, keepdims=True)
    cnt_ref[...] = carry_ref[...].astype(jnp.int32)

    idx_ref[...] = _lane_pack(idxs, lane)[:, :TOP_K]
    rank_ref[...] = _lane_pack(ranks, lane)[:, :TOP_K]
    gate_ref[...] = _lane_pack(gates, lane)[:, :TOP_K]


def _postmix(x2d, attn2d, ga, gp, w_attn_out, w_o, g_ffn, w_router, b_router):
    t = x2d.shape[0]
    tm = TOKEN_TILE
    f32 = jnp.float32
    wr = jnp.zeros((D_MODEL, LANES), f32).at[:, :N_EXPERTS].set(w_router)
    wr_hi = wr.astype(MXU_DTYPE)
    wr_lo = (wr - wr_hi.astype(f32)).astype(MXU_DTYPE)
    br = jnp.full((1, LANES), NEG_BIG, f32).at[0, :N_EXPERTS].set(b_router)
    weights = (w_attn_out.astype(MXU_DTYPE), w_o.astype(MXU_DTYPE), g_ffn[None, :], wr_hi, wr_lo, br)
    row_spec = lambda width: pl.BlockSpec((tm, width), lambda i: (i, 0))
    return pl.pallas_call(
        _postmix_kernel,
        grid=(t // tm,),
        in_specs=[row_spec(D_MODEL), row_spec(N_HEADS * V_HEAD), row_spec(D_MODEL), row_spec(D_MODEL)]
        + [_full_spec(w) for w in weights],
        out_specs=(row_spec(D_MODEL), row_spec(D_MODEL), row_spec(TOP_K), row_spec(TOP_K), row_spec(TOP_K),
                   pl.BlockSpec((1, LANES), lambda i: (0, 0))),
        out_shape=(
            jax.ShapeDtypeStruct((t, D_MODEL), f32),
            jax.ShapeDtypeStruct((t, D_MODEL), f32),
            jax.ShapeDtypeStruct((t, TOP_K), jnp.int32),
            jax.ShapeDtypeStruct((t, TOP_K), jnp.int32),
            jax.ShapeDtypeStruct((t, TOP_K), f32),
            jax.ShapeDtypeStruct((1, LANES), jnp.int32),
        ),
        scratch_shapes=[pltpu.VMEM((1, LANES), f32)],
        compiler_params=pltpu.CompilerParams(
            dimension_semantics=("arbitrary",), vmem_limit_bytes=VMEM_LIMIT_BYTES),
        name="postmix",
    )(x2d, attn2d, ga, gp, *weights)


def _row_copy_wait(src_ref, dst_ref, sem):
    pltpu.make_async_copy(src_ref, dst_ref, sem).wait()


def _dispatch_kernel(dest_ref, h2_ref, xs_ref, sem):
    tm = h2_ref.shape[0]
    base = pl.program_id(0) * (tm * TOP_K)

    def issue(t, carry):
        for k in range(TOP_K):
            d = dest_ref[base + t * TOP_K + k]
            pltpu.make_async_copy(h2_ref.at[pl.ds(t, 1)], xs_ref.at[pl.ds(d, 1)], sem).start()
        return carry

    lax.fori_loop(0, tm, issue, 0, unroll=8)
    for _ in range(TOP_K):
        _row_copy_wait(h2_ref, xs_ref.at[pl.ds(0, tm)], sem)


def _dispatch(dest_flat, h2, total_rows):
    t = h2.shape[0]
    tm = TOKEN_TILE
    return pl.pallas_call(
        _dispatch_kernel,
        grid_spec=pltpu.PrefetchScalarGridSpec(
            num_scalar_prefetch=1,
            grid=(t // tm,),
            in_specs=[pl.BlockSpec((tm, D_MODEL), lambda i, dest: (i, 0))],
            out_specs=pl.BlockSpec(memory_space=pl.ANY),
            scratch_shapes=[pltpu.SemaphoreType.DMA(())],
        ),
        out_shape=jax.ShapeDtypeStruct((total_rows, D_MODEL), jnp.float32),
        compiler_params=pltpu.CompilerParams(
            dimension_semantics=("arbitrary",), vmem_limit_bytes=VMEM_LIMIT_BYTES, has_side_effects=True),
        name="dispatch",
    )(dest_flat, h2)


def _experts_kernel(blk_ref, bexp_ref, nval_ref, first_ref, xs_ref, wgu_ref, bgu_ref, wd_ref, bd_ref,
                    y_ref, wgu_bf, wd_bf):
    i = pl.program_id(0)
    rows = xs_ref.shape[0]

    @pl.when(first_ref[i] == 1)
    def _():
        wgu_bf[...] = wgu_ref[0].astype(MXU_DTYPE)
        wd_bf[...] = wd_ref[0].astype(MXU_DTYPE)

    @pl.when(nval_ref[i] > 0)
    def _():
        row = lax.broadcasted_iota(jnp.int32, (rows, 1), 0)
        x = jnp.where(row < nval_ref[i], xs_ref[...], 0.0).astype(MXU_DTYPE)
        gu = _dot(x, wgu_bf[...]) + bgu_ref[0]
        gate = jnp.minimum(gu[:, :D_EXPERT], SWIGLU_LIMIT)
        up = jnp.clip(gu[:, D_EXPERT:], -SWIGLU_LIMIT, SWIGLU_LIMIT)
        act = (up + 1.0) * gate * jax.nn.sigmoid(SWIGLU_ALPHA * gate)
        y_ref[...] = _dot(act.astype(MXU_DTYPE), wd_bf[...]) + bd_ref[0]


def _experts(blk, bexp, nval, first, xs, w_gate_up, b_gate_up, w_down, b_down):
    rows = EXPERT_ROWS
    nb = blk.shape[0]
    return pl.pallas_call(
        _experts_kernel,
        grid_spec=pltpu.PrefetchScalarGridSpec(
            num_scalar_prefetch=4,
            grid=(nb,),
            in_specs=[
                pl.BlockSpec((rows, D_MODEL), lambda i, blk, bexp, nval, first: (blk[i], 0)),
                pl.BlockSpec((1, D_MODEL, 2 * D_EXPERT), lambda i, blk, bexp, nval, first: (bexp[i], 0, 0)),
                pl.BlockSpec((1, 1, 2 * D_EXPERT), lambda i, blk, bexp, nval, first: (bexp[i], 0, 0)),
                pl.BlockSpec((1, D_EXPERT, D_MODEL), lambda i, blk, bexp, nval, first: (bexp[i], 0, 0)),
                pl.BlockSpec((1, 1, D_MODEL), lambda i, blk, bexp, nval, first: (bexp[i], 0, 0)),
            ],
            out_specs=pl.BlockSpec((rows, D_MODEL), lambda i, blk, bexp, nval, first: (blk[i], 0)),
            scratch_shapes=[pltpu.VMEM((D_MODEL, 2 * D_EXPERT), MXU_DTYPE),
                            pltpu.VMEM((D_EXPERT, D_MODEL), MXU_DTYPE)],
        ),
        out_shape=jax.ShapeDtypeStruct(xs.shape, jnp.float32),
        compiler_params=pltpu.CompilerParams(
            dimension_semantics=("arbitrary",), vmem_limit_bytes=VMEM_LIMIT_BYTES),
        name="experts",
    )(blk, bexp, nval, first, xs, w_gate_up, b_gate_up[:, None, :], w_down, b_down[:, None, :])


def _expert_blocks(counts, tokens):
    rows = EXPERT_ROWS
    nb_max = (tokens * TOP_K) // rows + N_EXPERTS
    nblk = (counts + rows - 1) // rows
    ends = jnp.cumsum(nblk)
    total = ends[-1]
    i = jnp.minimum(jnp.arange(nb_max, dtype=jnp.int32), total - 1)
    e = jnp.minimum(jnp.sum(ends[None, :] <= i[:, None], axis=1), N_EXPERTS - 1).astype(jnp.int32)
    j = i - (ends[e] - nblk[e])
    active = jnp.arange(nb_max, dtype=jnp.int32) < total
    blk = e * (tokens // rows) + j
    nval = jnp.where(active, jnp.clip(counts[e] - j * rows, 0, rows), 0)
    first = jnp.where(active & (j == 0), 1, 0)
    return blk.astype(jnp.int32), e, nval.astype(jnp.int32), first.astype(jnp.int32)


def _combine_kernel(dest_ref, gate_ref, x1_ref, p_ref, y_ref, gple_ref, wpg_ref, wple_ref,
                    o_ref, buf, sems):
    tm = x1_ref.shape[0]
    i = pl.program_id(0)
    n = pl.num_programs(0)

    def issue(tile, slot):
        base = tile * (tm * TOP_K)

        def body(t, carry):
            for k in range(TOP_K):
                d = dest_ref[base + t * TOP_K + k]
                pltpu.make_async_copy(y_ref.at[pl.ds(d, 1)], buf.at[slot, k, pl.ds(t, 1)], sems.at[slot]).start()
            return carry

        lax.fori_loop(0, tm, body, 0, unroll=8)

    @pl.when(i == 0)
    def _():
        issue(0, 0)

    @pl.when(i + 1 < n)
    def _():
        issue(i + 1, (i + 1) % 2)

    slot = i % 2
    for k in range(TOP_K):
        _row_copy_wait(y_ref.at[pl.ds(0, tm)], buf.at[slot, k], sems.at[slot])

    gates = gate_ref[...]
    moe = gates[:, 0:1] * buf[slot, 0]
    for k in range(1, TOP_K):
        moe = moe + gates[:, k:k + 1] * buf[slot, k]
    x2 = x1_ref[...] + moe
    hp = (x2 * _rms_scale(x2, D_MODEL) * gple_ref[...]).astype(MXU_DTYPE)
    ple_gate = jax.nn.sigmoid(_dot(hp, wpg_ref[...]))
    o_ref[...] = x2 + ple_gate * _dot(p_ref[...].astype(MXU_DTYPE), wple_ref[...])


def _combine(dest_flat, gates, x1, p2d, y, g_ple, w_ple_gate, w_ple):
    t = x1.shape[0]
    tm = TOKEN_TILE
    weights = (g_ple[None, :], w_ple_gate.astype(MXU_DTYPE), w_ple.astype(MXU_DTYPE))
    return pl.pallas_call(
        _combine_kernel,
        grid_spec=pltpu.PrefetchScalarGridSpec(
            num_scalar_prefetch=1,
            grid=(t // tm,),
            in_specs=[
                pl.BlockSpec((tm, TOP_K), lambda i, dest: (i, 0)),
                pl.BlockSpec((tm, D_MODEL), lambda i, dest: (i, 0)),
                pl.BlockSpec((tm, PLE_DIM), lambda i, dest: (i, 0)),
                pl.BlockSpec(memory_space=pl.ANY),
            ] + [pl.BlockSpec(w.shape, lambda i, dest, _nd=w.ndim: (0,) * _nd) for w in weights],
            out_specs=pl.BlockSpec((tm, D_MODEL), lambda i, dest: (i, 0)),
            scratch_shapes=[pltpu.VMEM((2, TOP_K, tm, D_MODEL), jnp.float32),
                            pltpu.SemaphoreType.DMA((2,))],
        ),
        out_shape=jax.ShapeDtypeStruct((t, D_MODEL), jnp.float32),
        compiler_params=pltpu.CompilerParams(
            dimension_semantics=("arbitrary",), vmem_limit_bytes=VMEM_LIMIT_BYTES),
        name="combine",
    )(dest_flat, gates, x1, p2d, y, *weights)


def kernel(x, p, positions, g_mix, w_in, g_q_lat, w_q_b, g_kv_lat, w_kv_b, g_q_head, g_k_nope, g_k_rope, w_pool, pool_scale, w_attn_out, w_pool_out, w_o, g_ffn, w_router, b_router, w_gate_up, b_gate_up, w_down, b_down, g_ple, w_ple_gate, w_ple):
    batch, seq, _ = x.shape
    tokens = batch * seq
    depth = g_mix.shape[0]
    assert seq % TOKEN_TILE == 0 and seq % ATTN_TILE == 0 and tokens % EXPERT_ROWS == 0
    x2d = x.reshape(tokens, D_MODEL)
    pos2d = positions.astype(jnp.float32).reshape(tokens, 1)
    for i in range(depth):
        weights = _premix_weights(g_mix[i], w_in[i], g_q_lat[i], w_q_b[i], g_kv_lat[i], w_kv_b[i],
                                  g_q_head[i], g_k_nope[i], g_k_rope[i], w_pool[i], pool_scale[i],
                                  w_pool_out[i])
        q, kt, v, ga, gp = _premix(x2d, pos2d, weights, batch, seq)
        attn = _attention(q, kt, v).reshape(tokens, N_HEADS * V_HEAD)
        x1, h2, idx, rank, gates, counts = _postmix(x2d, attn, ga, gp, w_attn_out[i], w_o[i], g_ffn[i],
                                                    w_router[i], b_router[i])
        dest = (idx * tokens + rank).reshape(-1)
        blk, bexp, nval, first = _expert_blocks(counts[0, :N_EXPERTS], tokens)
        xs = _dispatch(dest, h2, N_EXPERTS * tokens)
        y = _experts(blk, bexp, nval, first, xs, w_gate_up[i], b_gate_up[i], w_down[i], b_down[i])
        x2d = _combine(dest, gates, x1, p[i].reshape(tokens, PLE_DIM), y, g_ple[i], w_ple_gate[i], w_ple[i])
    return x2d.reshape(batch, seq, D_MODEL)
```

```python
import functools
import math

import jax
import jax.numpy as jnp
import numpy as np
from jax import lax
from jax.experimental import pallas as pl
from jax.experimental.pallas import tpu as pltpu

D_MODEL = 1024
N_HEADS = 8
QK_NOPE = 64
QK_ROPE = 32
QK_HEAD = QK_NOPE + QK_ROPE
V_HEAD = 64
Q_LORA = 256
KV_LORA = 128
ROPE_THETA = 10000.0
EPS = 1e-6
POOL_WINDOWS = (2, 4, 8, 16)
POOL_GROUPS = 4
POOL_WIDTH = 512
POOL_GROUP_DIM = POOL_WIDTH // POOL_GROUPS
POOL_HALO = 16
N_EXPERTS = 32
TOP_K = 4
D_EXPERT = 1024
SWIGLU_ALPHA = 1.702
SWIGLU_LIMIT = 7.0
PLE_DIM = 256

LANES = 128
HEAD_SLAB = LANES
VMEM_LIMIT_BYTES = 56 * 1024 * 1024

MXU_DTYPE = jnp.bfloat16
NEG_BIG = -1e30
Q_SCALE = math.log2(math.e) / math.sqrt(QK_HEAD)

TOKEN_TILE = 512
ATTN_TILE = 512
EXPERT_ROWS = 512


def _dot(a, b):
    return jnp.dot(a, b, preferred_element_type=jnp.float32)


def _rms_scale(v, width):
    return lax.rsqrt(jnp.sum(v * v, axis=-1, keepdims=True) * (1.0 / width) + EPS)


_C_QLAT = 0
_C_KVLAT = _C_QLAT + Q_LORA
_C_KR = _C_KVLAT + KV_LORA
_C_KRR = _C_KR + HEAD_SLAB
_C_POOL = _C_KRR + HEAD_SLAB
_C_GA = _C_POOL + POOL_WIDTH
_C_GP = _C_GA + D_MODEL
_C_END = _C_GP + D_MODEL


def _premix_kernel(tiles_per_seq, x_ref, pos_ref, gmix_ref, w1_ref, gql_ref, wqs_ref, wqr_ref,
                   gkvl_ref, wks_ref, wv_ref, gqs_ref, gqr_ref, gks_ref, gkrs_ref, gkrr_ref,
                   invf_ref, wpool_ref, pscale_ref, wpo_ref,
                   q_ref, kt_ref, v_ref, ga_ref, gp_ref, halo_ref):
    tm = x_ref.shape[0]
    si = pl.program_id(0) % tiles_per_seq

    xv = x_ref[...]
    h = xv * _rms_scale(xv, D_MODEL) * gmix_ref[...]
    proj = _dot(h.astype(MXU_DTYPE), w1_ref[...])

    ang = pos_ref[...] * invf_ref[...]
    cos_t = jnp.cos(ang)
    sin_t = jnp.sin(ang)
    lane = lax.broadcasted_iota(jnp.int32, (tm, LANES), 1)
    cos_q = jnp.where(lane < QK_NOPE, 1.0, cos_t)

    q_lat = proj[:, _C_QLAT:_C_QLAT + Q_LORA]
    qln = (q_lat * _rms_scale(q_lat, Q_LORA) * gql_ref[...]).astype(MXU_DTYPE)
    qs = _dot(qln, wqs_ref[...])
    qr = _dot(qln, wqr_ref[...])
    q_cos = gqs_ref[...] * cos_q
    q_sin = gqr_ref[...] * sin_t
    for hd in range(N_HEADS):
        s = qs[:, hd * HEAD_SLAB:(hd + 1) * HEAD_SLAB]
        r = qr[:, hd * HEAD_SLAB:(hd + 1) * HEAD_SLAB]
        scale = _rms_scale(s, QK_HEAD) * Q_SCALE
        q_ref[0, hd] = ((s * q_cos + r * q_sin) * scale).astype(q_ref.dtype)

    kv_lat = proj[:, _C_KVLAT:_C_KVLAT + KV_LORA]
    kvn = (kv_lat * _rms_scale(kv_lat, KV_LORA) * gkvl_ref[...]).astype(MXU_DTYPE)
    ks = _dot(kvn, wks_ref[...])
    v_ref[0] = _dot(kvn, wv_ref[...]).astype(v_ref.dtype)
    krs = proj[:, _C_KR:_C_KR + HEAD_SLAB]
    krr = proj[:, _C_KRR:_C_KRR + HEAD_SLAB]
    k_rot = (krs * (gkrs_ref[...] * cos_t) + krr * (gkrr_ref[...] * sin_t)) * _rms_scale(krs, QK_ROPE)
    for hd in range(N_HEADS):
        s = ks[:, hd * HEAD_SLAB:(hd + 1) * HEAD_SLAB]
        kh = s * _rms_scale(s, QK_NOPE) * gks_ref[...] + k_rot
        kt_ref[0, hd] = kh.T.astype(kt_ref.dtype)

    u = proj[:, _C_POOL:_C_POOL + POOL_WIDTH]

    @pl.when(si == 0)
    def _():
        halo_ref[...] = jnp.zeros_like(halo_ref)

    ext = jnp.concatenate([halo_ref[...], u], axis=0)
    halo_ref[...] = u[tm - POOL_HALO:, :]
    t_seq = si * tm + lax.broadcasted_iota(jnp.int32, (tm, 1), 0)
    mixed = []
    for g, w in enumerate(POOL_WINDOWS):
        acc = ext[:, g * POOL_GROUP_DIM:(g + 1) * POOL_GROUP_DIM]
        shift = 1
        while shift < w:
            acc = acc + pltpu.roll(acc, shift, axis=0)
            shift *= 2
        win = acc[POOL_HALO:, :]
        count = jnp.minimum(t_seq + 1, w).astype(jnp.float32)
        pooled = win / count - u[:, g * POOL_GROUP_DIM:(g + 1) * POOL_GROUP_DIM]
        mixed.append(_dot(pooled.astype(MXU_DTYPE), wpool_ref[g]))
    mixed = jnp.concatenate(mixed, axis=1) * pscale_ref[...]
    y_pool = _dot(mixed.astype(MXU_DTYPE), wpo_ref[...])

    ga_ref[...] = jax.nn.sigmoid(proj[:, _C_GA:_C_GA + D_MODEL]).astype(ga_ref.dtype)
    gp_ref[...] = (jax.nn.sigmoid(proj[:, _C_GP:_C_GP + D_MODEL]) * y_pool).astype(gp_ref.dtype)


def _slab_cols(width_per_head):
    return (np.arange(N_HEADS)[:, None] * HEAD_SLAB + np.arange(width_per_head)[None, :]).reshape(-1)


def _premix_weights(g_mix, w_in, g_q_lat, w_q_b, g_kv_lat, w_kv_b, g_q_head, g_k_nope, g_k_rope,
                    w_pool, pool_scale, w_pool_out):
    f32 = jnp.float32
    half = QK_ROPE // 2
    w_q_lat = w_in[:, :Q_LORA]
    w_kv_lat = w_in[:, Q_LORA:Q_LORA + KV_LORA]
    w_kr = w_in[:, Q_LORA + KV_LORA:Q_LORA + KV_LORA + QK_ROPE]
    off = Q_LORA + KV_LORA + QK_ROPE
    w_pool_in = w_in[:, off:off + POOL_WIDTH]
    w_ga = w_in[:, off + POOL_WIDTH:off + POOL_WIDTH + D_MODEL]
    w_gp = w_in[:, off + POOL_WIDTH + D_MODEL:]

    kr_s = jnp.zeros((D_MODEL, HEAD_SLAB), f32).at[:, QK_NOPE:QK_NOPE + QK_ROPE].set(w_kr)
    kr_r = jnp.zeros((D_MODEL, HEAD_SLAB), f32)
    kr_r = kr_r.at[:, QK_NOPE:QK_NOPE + half].set(-w_kr[:, half:])
    kr_r = kr_r.at[:, QK_NOPE + half:QK_NOPE + QK_ROPE].set(w_kr[:, :half])
    w1 = jnp.concatenate([w_q_lat, w_kv_lat, kr_s, kr_r, w_pool_in, w_ga, w_gp], axis=1)

    wq = w_q_b.reshape(Q_LORA, N_HEADS, QK_HEAD)
    wq_s = jnp.zeros((Q_LORA, N_HEADS, HEAD_SLAB), f32).at[:, :, :QK_HEAD].set(wq)
    wq_r = jnp.zeros((Q_LORA, N_HEADS, HEAD_SLAB), f32)
    wq_r = wq_r.at[:, :, QK_NOPE:QK_NOPE + half].set(-wq[:, :, QK_NOPE + half:])
    wq_r = wq_r.at[:, :, QK_NOPE + half:QK_HEAD].set(wq[:, :, QK_NOPE:QK_NOPE + half])
    wkv = w_kv_b.reshape(KV_LORA, N_HEADS, QK_NOPE + V_HEAD)
    wk_s = jnp.zeros((KV_LORA, N_HEADS, HEAD_SLAB), f32).at[:, :, :QK_NOPE].set(wkv[:, :, :QK_NOPE])
    wv = wkv[:, :, QK_NOPE:].reshape(KV_LORA, N_HEADS * V_HEAD)

    def slab(vals, start):
        return jnp.zeros((1, HEAD_SLAB), f32).at[0, start:start + vals.shape[0]].set(vals)

    gq_s = slab(g_q_head, 0)
    gq_r = slab(jnp.concatenate([g_q_head[QK_NOPE + half:], g_q_head[QK_NOPE:QK_NOPE + half]]), QK_NOPE)
    gk_s = slab(g_k_nope, 0)
    gkr_s = slab(g_k_rope, QK_NOPE)
    gkr_r = slab(jnp.concatenate([g_k_rope[half:], g_k_rope[:half]]), QK_NOPE)
    inv_freq = ROPE_THETA ** (-jnp.arange(0, QK_ROPE, 2, dtype=f32) / QK_ROPE)
    invf = jnp.tile(inv_freq, LANES // half)[None, :]
    bf = MXU_DTYPE
    return (g_mix[None, :], w1.astype(bf), g_q_lat[None, :],
            wq_s.reshape(Q_LORA, -1).astype(bf), wq_r.reshape(Q_LORA, -1).astype(bf),
            g_kv_lat[None, :], wk_s.reshape(KV_LORA, -1).astype(bf), wv.astype(bf),
            gq_s, gq_r, gk_s, gkr_s, gkr_r, invf, w_pool.astype(bf), pool_scale[None, :],
            w_pool_out.astype(bf))


def _full_spec(arr):
    nd = arr.ndim
    return pl.BlockSpec(arr.shape, lambda i, _nd=nd: (0,) * _nd)


def _premix(x2d, pos2d, weights, batch, seq):
    t = x2d.shape[0]
    tm = TOKEN_TILE
    tiles_per_seq = seq // tm
    in_specs = [pl.BlockSpec((tm, D_MODEL), lambda i: (i, 0)),
                pl.BlockSpec((tm, 1), lambda i: (i, 0))] + [_full_spec(w) for w in weights]
    out_shape = (
        jax.ShapeDtypeStruct((batch, N_HEADS, seq, HEAD_SLAB), MXU_DTYPE),
        jax.ShapeDtypeStruct((batch, N_HEADS, HEAD_SLAB, seq), MXU_DTYPE),
        jax.ShapeDtypeStruct((batch, seq, N_HEADS * V_HEAD), MXU_DTYPE),
        jax.ShapeDtypeStruct((t, D_MODEL), MXU_DTYPE),
        jax.ShapeDtypeStruct((t, D_MODEL), MXU_DTYPE),
    )
    out_specs = (
        pl.BlockSpec((1, N_HEADS, tm, HEAD_SLAB), lambda i: (i // tiles_per_seq, 0, i % tiles_per_seq, 0)),
        pl.BlockSpec((1, N_HEADS, HEAD_SLAB, tm), lambda i: (i // tiles_per_seq, 0, 0, i % tiles_per_seq)),
        pl.BlockSpec((1, tm, N_HEADS * V_HEAD), lambda i: (i // tiles_per_seq, i % tiles_per_seq, 0)),
        pl.BlockSpec((tm, D_MODEL), lambda i: (i, 0)),
        pl.BlockSpec((tm, D_MODEL), lambda i: (i, 0)),
    )
    return pl.pallas_call(
        functools.partial(_premix_kernel, tiles_per_seq),
        grid=(t // tm,),
        in_specs=in_specs,
        out_specs=out_specs,
        out_shape=out_shape,
        scratch_shapes=[pltpu.VMEM((POOL_HALO, POOL_WIDTH), jnp.float32)],
        compiler_params=pltpu.CompilerParams(
            dimension_semantics=("arbitrary",), vmem_limit_bytes=VMEM_LIMIT_BYTES),
        name="premix",
    )(x2d, pos2d, *weights)


def _attention_kernel(q_ref, kt_ref, v_ref, o_ref, va_ref, vb_ref, s_ref, m_ref, acc_ref):
    tq = q_ref.shape[2]
    tk = tq
    qi = pl.program_id(2)

    ones_lane = (V_HEAD, 0)

    @pl.when(qi == 0)
    def _():
        v = v_ref[0].astype(jnp.float32)
        lane = lax.broadcasted_iota(jnp.int32, v.shape, 1)
        pad_a = (lane == ones_lane[0]).astype(jnp.float32)
        pad_b = (lane == ones_lane[1]).astype(jnp.float32)
        va_ref[...] = jnp.where(lane < V_HEAD, v, pad_a).astype(va_ref.dtype)
        vb_ref[...] = jnp.where(lane >= V_HEAD, v, pad_b).astype(vb_ref.dtype)

    vsel = (va_ref, vb_ref)
    heads = range(2)

    def scores(blk, slot):
        start = pl.multiple_of(blk * tk, tk)
        for hh in heads:
            s_ref[slot, hh] = _dot(q_ref[0, hh], kt_ref[0, hh, :, pl.ds(start, tk)])

    def softmax_pv(blk, slot, masked):
        start = pl.multiple_of(blk * tk, tk)
        for hh in heads:
            if masked:
                row = lax.broadcasted_iota(jnp.int32, (tq, tk), 0)
                col = lax.broadcasted_iota(jnp.int32, (tq, tk), 1)
                s_ref[slot, hh] = jnp.where(row >= col, s_ref[slot, hh], NEG_BIG)
            m_old = m_ref[hh]
            m_new = jnp.maximum(m_old, jnp.max(s_ref[slot, hh], axis=-1, keepdims=True))
            p = jnp.exp2(s_ref[slot, hh] - m_new).astype(MXU_DTYPE)
            acc_ref[hh] = jnp.exp2(m_old - m_new) * acc_ref[hh] + _dot(p, vsel[hh][pl.ds(start, tk), :])
            m_ref[hh] = m_new

    m_ref[...] = jnp.full(m_ref.shape, NEG_BIG, jnp.float32)
    acc_ref[...] = jnp.zeros(acc_ref.shape, jnp.float32)
    scores(0, 0)

    def two_blocks(jj, carry):
        scores(2 * jj + 1, 1)
        softmax_pv(2 * jj, 0, False)
        scores(2 * jj + 2, 0)
        softmax_pv(2 * jj + 1, 1, False)
        return carry

    lax.fori_loop(0, qi // 2, two_blocks, 0)

    @pl.when(qi % 2 == 0)
    def _():
        softmax_pv(qi, 0, True)

    @pl.when(qi % 2 == 1)
    def _():
        scores(qi, 1)
        softmax_pv(qi - 1, 0, False)
        softmax_pv(qi, 1, True)

    lane = lax.broadcasted_iota(jnp.int32, (tq, HEAD_SLAB), 1)
    acc_a = acc_ref[0]
    acc_b = acc_ref[1]
    out_a = acc_a / acc_a[:, ones_lane[0]:ones_lane[0] + 1]
    out_b = acc_b / acc_b[:, ones_lane[1]:ones_lane[1] + 1]
    o_ref[0] = jnp.where(lane < V_HEAD, out_a, out_b).astype(o_ref.dtype)


def _attention(q, kt, v):
    batch, _, seq, _ = q.shape
    tq = ATTN_TILE
    pairs = N_HEADS // 2
    return pl.pallas_call(
        _attention_kernel,
        grid=(batch, pairs, seq // tq),
        in_specs=[
            pl.BlockSpec((1, 2, tq, HEAD_SLAB), lambda b, p, i: (b, p, i, 0)),
            pl.BlockSpec((1, 2, HEAD_SLAB, seq), lambda b, p, i: (b, p, 0, 0)),
            pl.BlockSpec((1, seq, 2 * V_HEAD), lambda b, p, i: (b, 0, p)),
        ],
        out_specs=pl.BlockSpec((1, tq, 2 * V_HEAD), lambda b, p, i: (b, i, p)),
        out_shape=jax.ShapeDtypeStruct((batch, seq, N_HEADS * V_HEAD), MXU_DTYPE),
        scratch_shapes=[pltpu.VMEM((seq, 2 * V_HEAD), MXU_DTYPE), pltpu.VMEM((seq, 2 * V_HEAD), MXU_DTYPE),
                        pltpu.VMEM((2, 2, tq, tq), jnp.float32),
                        pltpu.VMEM((2, tq, 1), jnp.float32),
                        pltpu.VMEM((2, tq, HEAD_SLAB), jnp.float32)],
        compiler_params=pltpu.CompilerParams(
            dimension_semantics=("arbitrary", "arbitrary", "arbitrary"), vmem_limit_bytes=VMEM_LIMIT_BYTES),
        name="attention",
    )(q, kt, v)


def _lane_pack(cols, lane):
    out = jnp.zeros(lane.shape, cols[0].dtype)
    for k, c in enumerate(cols):
        out = jnp.where(lane == k, c, out)
    return out


def _postmix_kernel(x_ref, at_ref, ga_ref, gp_ref, wao_ref, wo_ref, gffn_ref, wrh_ref, wrl_ref, br_ref,
                    x1_ref, h2_ref, idx_ref, rank_ref, gate_ref, cnt_ref, carry_ref):
    tm = x_ref.shape[0]

    @pl.when(pl.program_id(0) == 0)
    def _():
        carry_ref[...] = jnp.zeros_like(carry_ref)

    y_attn = _dot(at_ref[...], wao_ref[...])
    merged = ga_ref[...].astype(jnp.float32) * y_attn + gp_ref[...].astype(jnp.float32)
    x1 = x_ref[...] + _dot(merged.astype(MXU_DTYPE), wo_ref[...])
    x1_ref[...] = x1
    h2 = x1 * _rms_scale(x1, D_MODEL) * gffn_ref[...]
    h2_ref[...] = h2

    h_hi = h2.astype(MXU_DTYPE)
    h_lo = (h2 - h_hi.astype(jnp.float32)).astype(MXU_DTYPE)
    logits = _dot(h_hi, wrh_ref[...]) + (_dot(h_lo, wrh_ref[...]) + _dot(h_hi, wrl_ref[...])) + br_ref[...]

    lane = lax.broadcasted_iota(jnp.int32, (tm, LANES), 1)
    work = logits
    vals, idxs = [], []
    for _ in range(TOP_K):
        mx = jnp.max(work, axis=-1, keepdims=True)
        ix = jnp.min(jnp.where(work == mx, lane, LANES), axis=-1, keepdims=True)
        vals.append(mx)
        idxs.append(ix)
        work = jnp.where(lane == ix, -jnp.inf, work)
    exps = [jnp.exp(v - vals[0]) for v in vals]
    denom = exps[0] + exps[1] + exps[2] + exps[3]
    gates = [e / denom for e in exps]

    onehot = jnp.zeros((tm, LANES), jnp.float32)
    for ix in idxs:
        onehot = onehot + (lane == ix).astype(jnp.float32)
    r_i = lax.broadcasted_iota(jnp.int32, (tm, tm), 0)
    c_i = lax.broadcasted_iota(jnp.int32, (tm, tm), 1)
    lower = (c_i < r_i).astype(MXU_DTYPE)
    before = carry_ref[...] + _dot(lower, onehot.astype(MXU_DTYPE))
    ranks = [jnp.sum(jnp.where(lane == ix, before, 0.0), axis=-1, keepdims=True).astype(jnp.int32)
             for ix in idxs]
    carry_ref[...] = carry_ref[...] + jnp.sum(onehot, axis=0, keepdims=True)
    cnt_ref[...] = carry_ref[...].astype(jnp.int32)

    idx_ref[...] = _lane_pack(idxs, lane)[:, :TOP_K]
    rank_ref[...] = _lane_pack(ranks, lane)[:, :TOP_K]
    gate_ref[...] = _lane_pack(gates, lane)[:, :TOP_K]


def _postmix(x2d, attn2d, ga, gp, w_attn_out, w_o, g_ffn, w_router, b_router):
    t = x2d.shape[0]
    tm = TOKEN_TILE
    f32 = jnp.float32
    wr = jnp.zeros((D_MODEL, LANES), f32).at[:, :N_EXPERTS].set(w_router)
    wr_hi = wr.astype(MXU_DTYPE)
    wr_lo = (wr - wr_hi.astype(f32)).astype(MXU_DTYPE)
    br = jnp.full((1, LANES), NEG_BIG, f32).at[0, :N_EXPERTS].set(b_router)
    weights = (w_attn_out.astype(MXU_DTYPE), w_o.astype(MXU_DTYPE), g_ffn[None, :], wr_hi, wr_lo, br)
    row_spec = lambda width: pl.BlockSpec((tm, width), lambda i: (i, 0))
    return pl.pallas_call(
        _postmix_kernel,
        grid=(t // tm,),
        in_specs=[row_spec(D_MODEL), row_spec(N_HEADS * V_HEAD), row_spec(D_MODEL), row_spec(D_MODEL)]
        + [_full_spec(w) for w in weights],
        out_specs=(row_spec(D_MODEL), row_spec(D_MODEL), row_spec(TOP_K), row_spec(TOP_K), row_spec(TOP_K),
                   pl.BlockSpec((1, LANES), lambda i: (0, 0))),
        out_shape=(
            jax.ShapeDtypeStruct((t, D_MODEL), f32),
            jax.ShapeDtypeStruct((t, D_MODEL), f32),
            jax.ShapeDtypeStruct((t, TOP_K), jnp.int32),
            jax.ShapeDtypeStruct((t, TOP_K), jnp.int32),
            jax.ShapeDtypeStruct((t, TOP_K), f32),
            jax.ShapeDtypeStruct((1, LANES), jnp.int32),
        ),
        scratch_shapes=[pltpu.VMEM((1, LANES), f32)],
        compiler_params=pltpu.CompilerParams(
            dimension_semantics=("arbitrary",), vmem_limit_bytes=VMEM_LIMIT_BYTES),
        name="postmix",
    )(x2d, attn2d, ga, gp, *weights)


def _row_copy_wait(src_ref, dst_ref, sem):
    pltpu.make_async_copy(src_ref, dst_ref, sem).wait()


def _dispatch_kernel(dest_ref, h2_ref, xs_ref, sem):
    tm = h2_ref.shape[0]
    base = pl.program_id(0) * (tm * TOP_K)

    def issue(t, carry):
        for k in range(TOP_K):
            d = dest_ref[base + t * TOP_K + k]
            pltpu.make_async_copy(h2_ref.at[pl.ds(t, 1)], xs_ref.at[pl.ds(d, 1)], sem).start()
        return carry

    lax.fori_loop(0, tm, issue, 0, unroll=8)
    for _ in range(TOP_K):
        _row_copy_wait(h2_ref, xs_ref.at[pl.ds(0, tm)], sem)


def _dispatch(dest_flat, h2, total_rows):
    t = h2.shape[0]
    tm = TOKEN_TILE
    return pl.pallas_call(
        _dispatch_kernel,
        grid_spec=pltpu.PrefetchScalarGridSpec(
            num_scalar_prefetch=1,
            grid=(t // tm,),
            in_specs=[pl.BlockSpec((tm, D_MODEL), lambda i, dest: (i, 0))],
            out_specs=pl.BlockSpec(memory_space=pl.ANY),
            scratch_shapes=[pltpu.SemaphoreType.DMA(())],
        ),
        out_shape=jax.ShapeDtypeStruct((total_rows, D_MODEL), jnp.float32),
        compiler_params=pltpu.CompilerParams(
            dimension_semantics=("arbitrary",), vmem_limit_bytes=VMEM_LIMIT_BYTES, has_side_effects=True),
        name="dispatch",
    )(dest_flat, h2)


def _experts_kernel(blk_ref, bexp_ref, nval_ref, first_ref, xs_ref, wgu_ref, bgu_ref, wd_ref, bd_ref,
                    y_ref, wgu_bf, wd_bf):
    i = pl.program_id(0)
    rows = xs_ref.shape[0]

    @pl.when(first_ref[i] == 1)
    def _():
        wgu_bf[...] = wgu_ref[0].astype(MXU_DTYPE)
        wd_bf[...] = wd_ref[0].astype(MXU_DTYPE)

    @pl.when(nval_ref[i] > 0)
    def _():
        row = lax.broadcasted_iota(jnp.int32, (rows, 1), 0)
        x = jnp.where(row < nval_ref[i], xs_ref[...], 0.0).astype(MXU_DTYPE)
        gu = _dot(x, wgu_bf[...]) + bgu_ref[0]
        gate = jnp.minimum(gu[:, :D_EXPERT], SWIGLU_LIMIT)
        up = jnp.clip(gu[:, D_EXPERT:], -SWIGLU_LIMIT, SWIGLU_LIMIT)
        act = (up + 1.0) * gate * jax.nn.sigmoid(SWIGLU_ALPHA * gate)
        y_ref[...] = _dot(act.astype(MXU_DTYPE), wd_bf[...]) + bd_ref[0]


def _experts(blk, bexp, nval, first, xs, w_gate_up, b_gate_up, w_down, b_down):
    rows = EXPERT_ROWS
    nb = blk.shape[0]
    return pl.pallas_call(
        _experts_kernel,
        grid_spec=pltpu.PrefetchScalarGridSpec(
            num_scalar_prefetch=4,
            grid=(nb,),
            in_specs=[
                pl.BlockSpec((rows, D_MODEL), lambda i, blk, bexp, nval, first: (blk[i], 0)),
                pl.BlockSpec((1, D_MODEL, 2 * D_EXPERT), lambda i, blk, bexp, nval, first: (bexp[i], 0, 0)),
                pl.BlockSpec((1, 1, 2 * D_EXPERT), lambda i, blk, bexp, nval, first: (bexp[i], 0, 0)),
                pl.BlockSpec((1, D_EXPERT, D_MODEL), lambda i, blk, bexp, nval, first: (bexp[i], 0, 0)),
                pl.BlockSpec((1, 1, D_MODEL), lambda i, blk, bexp, nval, first: (bexp[i], 0, 0)),
            ],
            out_specs=pl.BlockSpec((rows, D_MODEL), lambda i, blk, bexp, nval, first: (blk[i], 0)),
            scratch_shapes=[pltpu.VMEM((D_MODEL, 2 * D_EXPERT), MXU_DTYPE),
                            pltpu.VMEM((D_EXPERT, D_MODEL), MXU_DTYPE)],
        ),
        out_shape=jax.ShapeDtypeStruct(xs.shape, jnp.float32),
        compiler_params=pltpu.CompilerParams(
            dimension_semantics=("arbitrary",), vmem_limit_bytes=VMEM_LIMIT_BYTES),
        name="experts",
    )(blk, bexp, nval, first, xs, w_gate_up, b_gate_up[:, None, :], w_down, b_down[:, None, :])


def _expert_blocks(counts, tokens):
    rows = EXPERT_ROWS
    nb_max = (tokens * TOP_K) // rows + N_EXPERTS
    nblk = (counts + rows - 1) // rows
    ends = jnp.cumsum(nblk)
    starts = ends - nblk
    total = ends[-1]
    step = jnp.arange(nb_max, dtype=jnp.int32)
    i = jnp.minimum(step, total - 1)[:, None]
    owner = ((i >= starts[None, :]) & (i < ends[None, :])).astype(jnp.int32)
    pick = lambda v: jnp.sum(owner * v[None, :], axis=1)
    e = pick(jnp.arange(N_EXPERTS, dtype=jnp.int32))
    j = i[:, 0] - pick(starts)
    active = step < total
    blk = e * (tokens // rows) + j
    nval = jnp.where(active, jnp.clip(pick(counts) - j * rows, 0, rows), 0)
    first = jnp.where(active & (j == 0), 1, 0)
    return blk.astype(jnp.int32), e.astype(jnp.int32), nval.astype(jnp.int32), first.astype(jnp.int32)


def _combine_kernel(dest_ref, gate_ref, x1_ref, p_ref, y_ref, gple_ref, wpg_ref, wple_ref,
                    o_ref, buf, sems):
    tm = x1_ref.shape[0]
    i = pl.program_id(0)
    n = pl.num_programs(0)

    def issue(tile, slot):
        base = tile * (tm * TOP_K)

        def body(t, carry):
            for k in range(TOP_K):
                d = dest_ref[base + t * TOP_K + k]
                pltpu.make_async_copy(y_ref.at[pl.ds(d, 1)], buf.at[slot, k, pl.ds(t, 1)], sems.at[slot]).start()
            return carry

        lax.fori_loop(0, tm, body, 0, unroll=8)

    @pl.when(i == 0)
    def _():
        issue(0, 0)

    @pl.when(i + 1 < n)
    def _():
        issue(i + 1, (i + 1) % 2)

    slot = i % 2
    for k in range(TOP_K):
        _row_copy_wait(y_ref.at[pl.ds(0, tm)], buf.at[slot, k], sems.at[slot])

    gates = gate_ref[...]
    moe = gates[:, 0:1] * buf[slot, 0]
    for k in range(1, TOP_K):
        moe = moe + gates[:, k:k + 1] * buf[slot, k]
    x2 = x1_ref[...] + moe
    hp = (x2 * _rms_scale(x2, D_MODEL) * gple_ref[...]).astype(MXU_DTYPE)
    ple_gate = jax.nn.sigmoid(_dot(hp, wpg_ref[...]))
    o_ref[...] = x2 + ple_gate * _dot(p_ref[...].astype(MXU_DTYPE), wple_ref[...])


def _combine(dest_flat, gates, x1, p2d, y, g_ple, w_ple_gate, w_ple):
    t = x1.shape[0]
    tm = TOKEN_TILE
    weights = (g_ple[None, :], w_ple_gate.astype(MXU_DTYPE), w_ple.astype(MXU_DTYPE))
    return pl.pallas_call(
        _combine_kernel,
        grid_spec=pltpu.PrefetchScalarGridSpec(
            num_scalar_prefetch=1,
            grid=(t // tm,),
            in_specs=[
                pl.BlockSpec((tm, TOP_K), lambda i, dest: (i, 0)),
                pl.BlockSpec((tm, D_MODEL), lambda i, dest: (i, 0)),
                pl.BlockSpec((tm, PLE_DIM), lambda i, dest: (i, 0)),
                pl.BlockSpec(memory_space=pl.ANY),
            ] + [pl.BlockSpec(w.shape, lambda i, dest, _nd=w.ndim: (0,) * _nd) for w in weights],
            out_specs=pl.BlockSpec((tm, D_MODEL), lambda i, dest: (i, 0)),
            scratch_shapes=[pltpu.VMEM((2, TOP_K, tm, D_MODEL), jnp.float32),
                            pltpu.SemaphoreType.DMA((2,))],
        ),
        out_shape=jax.ShapeDtypeStruct((t, D_MODEL), jnp.float32),
        compiler_params=pltpu.CompilerParams(
            dimension_semantics=("arbitrary",), vmem_limit_bytes=VMEM_LIMIT_BYTES),
        name="combine",
    )(dest_flat, gates, x1, p2d, y, *weights)


def kernel(x, p, positions, g_mix, w_in, g_q_lat, w_q_b, g_kv_lat, w_kv_b, g_q_head, g_k_nope, g_k_rope, w_pool, pool_scale, w_attn_out, w_pool_out, w_o, g_ffn, w_router, b_router, w_gate_up, b_gate_up, w_down, b_down, g_ple, w_ple_gate, w_ple):
    batch, seq, _ = x.shape
    tokens = batch * seq
    depth = g_mix.shape[0]
    assert seq % TOKEN_TILE == 0 and seq % ATTN_TILE == 0 and tokens % EXPERT_ROWS == 0
    x2d = x.reshape(tokens, D_MODEL)
    pos2d = positions.astype(jnp.float32).reshape(tokens, 1)
    for i in range(depth):
        weights = _premix_weights(g_mix[i], w_in[i], g_q_lat[i], w_q_b[i], g_kv_lat[i], w_kv_b[i],
                                  g_q_head[i], g_k_nope[i], g_k_rope[i], w_pool[i], pool_scale[i],
                                  w_pool_out[i])
        q, kt, v, ga, gp = _premix(x2d, pos2d, weights, batch, seq)
        attn = _attention(q, kt, v).reshape(tokens, N_HEADS * V_HEAD)
        x1, h2, idx, rank, gates, counts = _postmix(x2d, attn, ga, gp, w_attn_out[i], w_o[i], g_ffn[i],
                                                    w_router[i], b_router[i])
        dest = (idx * tokens + rank).reshape(-1)
        blk, bexp, nval, first = _expert_blocks(counts[0, :N_EXPERTS], tokens)
        xs = _dispatch(dest, h2, N_EXPERTS * tokens)
        y = _experts(blk, bexp, nval, first, xs, w_gate_up[i], b_gate_up[i], w_down[i], b_down[i])
        x2d = _combine(dest, gates, x1, p[i].reshape(tokens, PLE_DIM), y, g_ple[i], w_ple_gate[i], w_ple[i])
    return x2d.reshape(batch, seq, D_MODEL)
```

```python
import functools
import math

import jax
import jax.numpy as jnp
import numpy as np
from jax import lax
from jax.experimental import pallas as pl
from jax.experimental.pallas import tpu as pltpu

D_MODEL = 1024
N_HEADS = 8
QK_NOPE = 64
QK_ROPE = 32
QK_HEAD = QK_NOPE + QK_ROPE
V_HEAD = 64
Q_LORA = 256
KV_LORA = 128
ROPE_THETA = 10000.0
EPS = 1e-6
POOL_WINDOWS = (2, 4, 8, 16)
POOL_GROUPS = 4
POOL_WIDTH = 512
POOL_GROUP_DIM = POOL_WIDTH // POOL_GROUPS
POOL_HALO = 16
N_EXPERTS = 32
TOP_K = 4
D_EXPERT = 1024
SWIGLU_ALPHA = 1.702
SWIGLU_LIMIT = 7.0
PLE_DIM = 256

LANES = 128
HEAD_SLAB = LANES
ROW_CHUNKS = D_MODEL // LANES
VMEM_LIMIT_BYTES = 56 * 1024 * 1024

MXU_DTYPE = jnp.bfloat16
NEG_BIG = -1e30
Q_SCALE = math.log2(math.e) / math.sqrt(QK_HEAD)

TOKEN_TILE = 512
ATTN_TILE = 512
EXPERT_ROWS = 512


def _dot(a, b):
    return jnp.dot(a, b, preferred_element_type=jnp.float32)


def _rms_scale(v, width):
    return lax.rsqrt(jnp.sum(v * v, axis=-1, keepdims=True) * (1.0 / width) + EPS)


_C_QLAT = 0
_C_KVLAT = _C_QLAT + Q_LORA
_C_KR = _C_KVLAT + KV_LORA
_C_KRR = _C_KR + HEAD_SLAB
_C_POOL = _C_KRR + HEAD_SLAB
_C_GA = _C_POOL + POOL_WIDTH
_C_GP = _C_GA + D_MODEL
_C_END = _C_GP + D_MODEL


def _premix_kernel(tiles_per_seq, x_ref, pos_ref, gmix_ref, w1_ref, gql_ref, wqs_ref, wqr_ref,
                   gkvl_ref, wks_ref, wv_ref, gqs_ref, gqr_ref, gks_ref, gkrs_ref, gkrr_ref,
                   invf_ref, wpool_ref, pscale_ref, wpo_ref,
                   q_ref, kt_ref, v_ref, ga_ref, gp_ref, halo_ref):
    tm = x_ref.shape[0]
    si = pl.program_id(0) % tiles_per_seq

    xv = x_ref[...]
    h = xv * _rms_scale(xv, D_MODEL) * gmix_ref[...]
    proj = _dot(h.astype(MXU_DTYPE), w1_ref[...])

    ang = pos_ref[...] * invf_ref[...]
    cos_t = jnp.cos(ang)
    sin_t = jnp.sin(ang)
    lane = lax.broadcasted_iota(jnp.int32, (tm, LANES), 1)
    cos_q = jnp.where(lane < QK_NOPE, 1.0, cos_t)

    q_lat = proj[:, _C_QLAT:_C_QLAT + Q_LORA]
    qln = (q_lat * _rms_scale(q_lat, Q_LORA) * gql_ref[...]).astype(MXU_DTYPE)
    qs = _dot(qln, wqs_ref[...])
    qr = _dot(qln, wqr_ref[...])
    q_cos = gqs_ref[...] * cos_q
    q_sin = gqr_ref[...] * sin_t
    for hd in range(N_HEADS):
        s = qs[:, hd * HEAD_SLAB:(hd + 1) * HEAD_SLAB]
        r = qr[:, hd * HEAD_SLAB:(hd + 1) * HEAD_SLAB]
        scale = _rms_scale(s, QK_HEAD) * Q_SCALE
        q_ref[0, hd] = ((s * q_cos + r * q_sin) * scale).astype(q_ref.dtype)

    kv_lat = proj[:, _C_KVLAT:_C_KVLAT + KV_LORA]
    kvn = (kv_lat * _rms_scale(kv_lat, KV_LORA) * gkvl_ref[...]).astype(MXU_DTYPE)
    ks = _dot(kvn, wks_ref[...])
    v_ref[0] = _dot(kvn, wv_ref[...]).astype(v_ref.dtype)
    krs = proj[:, _C_KR:_C_KR + HEAD_SLAB]
    krr = proj[:, _C_KRR:_C_KRR + HEAD_SLAB]
    k_rot = (krs * (gkrs_ref[...] * cos_t) + krr * (gkrr_ref[...] * sin_t)) * _rms_scale(krs, QK_ROPE)
    for hd in range(N_HEADS):
        s = ks[:, hd * HEAD_SLAB:(hd + 1) * HEAD_SLAB]
        kh = s * _rms_scale(s, QK_NOPE) * gks_ref[...] + k_rot
        kt_ref[0, hd] = kh.T.astype(kt_ref.dtype)

    u = proj[:, _C_POOL:_C_POOL + POOL_WIDTH]

    @pl.when(si == 0)
    def _():
        halo_ref[...] = jnp.zeros_like(halo_ref)

    ext = jnp.concatenate([halo_ref[...], u], axis=0)
    halo_ref[...] = u[tm - POOL_HALO:, :]
    t_seq = si * tm + lax.broadcasted_iota(jnp.int32, (tm, 1), 0)
    mixed = []
    for g, w in enumerate(POOL_WINDOWS):
        acc = ext[:, g * POOL_GROUP_DIM:(g + 1) * POOL_GROUP_DIM]
        shift = 1
        while shift < w:
            acc = acc + pltpu.roll(acc, shift, axis=0)
            shift *= 2
        win = acc[POOL_HALO:, :]
        count = jnp.minimum(t_seq + 1, w).astype(jnp.float32)
        pooled = win / count - u[:, g * POOL_GROUP_DIM:(g + 1) * POOL_GROUP_DIM]
        mixed.append(_dot(pooled.astype(MXU_DTYPE), wpool_ref[g]))
    mixed = jnp.concatenate(mixed, axis=1) * pscale_ref[...]
    y_pool = _dot(mixed.astype(MXU_DTYPE), wpo_ref[...])

    ga_ref[...] = jax.nn.sigmoid(proj[:, _C_GA:_C_GA + D_MODEL]).astype(ga_ref.dtype)
    gp_ref[...] = (jax.nn.sigmoid(proj[:, _C_GP:_C_GP + D_MODEL]) * y_pool).astype(gp_ref.dtype)


def _slab_cols(width_per_head):
    return (np.arange(N_HEADS)[:, None] * HEAD_SLAB + np.arange(width_per_head)[None, :]).reshape(-1)


def _premix_weights(g_mix, w_in, g_q_lat, w_q_b, g_kv_lat, w_kv_b, g_q_head, g_k_nope, g_k_rope,
                    w_pool, pool_scale, w_pool_out):
    f32 = jnp.float32
    half = QK_ROPE // 2
    w_q_lat = w_in[:, :Q_LORA]
    w_kv_lat = w_in[:, Q_LORA:Q_LORA + KV_LORA]
    w_kr = w_in[:, Q_LORA + KV_LORA:Q_LORA + KV_LORA + QK_ROPE]
    off = Q_LORA + KV_LORA + QK_ROPE
    w_pool_in = w_in[:, off:off + POOL_WIDTH]
    w_ga = w_in[:, off + POOL_WIDTH:off + POOL_WIDTH + D_MODEL]
    w_gp = w_in[:, off + POOL_WIDTH + D_MODEL:]

    kr_s = jnp.zeros((D_MODEL, HEAD_SLAB), f32).at[:, QK_NOPE:QK_NOPE + QK_ROPE].set(w_kr)
    kr_r = jnp.zeros((D_MODEL, HEAD_SLAB), f32)
    kr_r = kr_r.at[:, QK_NOPE:QK_NOPE + half].set(-w_kr[:, half:])
    kr_r = kr_r.at[:, QK_NOPE + half:QK_NOPE + QK_ROPE].set(w_kr[:, :half])
    w1 = jnp.concatenate([w_q_lat, w_kv_lat, kr_s, kr_r, w_pool_in, w_ga, w_gp], axis=1)

    wq = w_q_b.reshape(Q_LORA, N_HEADS, QK_HEAD)
    wq_s = jnp.zeros((Q_LORA, N_HEADS, HEAD_SLAB), f32).at[:, :, :QK_HEAD].set(wq)
    wq_r = jnp.zeros((Q_LORA, N_HEADS, HEAD_SLAB), f32)
    wq_r = wq_r.at[:, :, QK_NOPE:QK_NOPE + half].set(-wq[:, :, QK_NOPE + half:])
    wq_r = wq_r.at[:, :, QK_NOPE + half:QK_HEAD].set(wq[:, :, QK_NOPE:QK_NOPE + half])
    wkv = w_kv_b.reshape(KV_LORA, N_HEADS, QK_NOPE + V_HEAD)
    wk_s = jnp.zeros((KV_LORA, N_HEADS, HEAD_SLAB), f32).at[:, :, :QK_NOPE].set(wkv[:, :, :QK_NOPE])
    wv = wkv[:, :, QK_NOPE:].reshape(KV_LORA, N_HEADS * V_HEAD)

    def slab(vals, start):
        return jnp.zeros((1, HEAD_SLAB), f32).at[0, start:start + vals.shape[0]].set(vals)

    gq_s = slab(g_q_head, 0)
    gq_r = slab(jnp.concatenate([g_q_head[QK_NOPE + half:], g_q_head[QK_NOPE:QK_NOPE + half]]), QK_NOPE)
    gk_s = slab(g_k_nope, 0)
    gkr_s = slab(g_k_rope, QK_NOPE)
    gkr_r = slab(jnp.concatenate([g_k_rope[half:], g_k_rope[:half]]), QK_NOPE)
    inv_freq = ROPE_THETA ** (-jnp.arange(0, QK_ROPE, 2, dtype=f32) / QK_ROPE)
    invf = jnp.tile(inv_freq, LANES // half)[None, :]
    bf = MXU_DTYPE
    return (g_mix[None, :], w1.astype(bf), g_q_lat[None, :],
            wq_s.reshape(Q_LORA, -1).astype(bf), wq_r.reshape(Q_LORA, -1).astype(bf),
            g_kv_lat[None, :], wk_s.reshape(KV_LORA, -1).astype(bf), wv.astype(bf),
            gq_s, gq_r, gk_s, gkr_s, gkr_r, invf, w_pool.astype(bf), pool_scale[None, :],
            w_pool_out.astype(bf))


def _full_spec(arr):
    nd = arr.ndim
    return pl.BlockSpec(arr.shape, lambda i, _nd=nd: (0,) * _nd)


def _premix(x2d, pos2d, weights, batch, seq):
    t = x2d.shape[0]
    tm = TOKEN_TILE
    tiles_per_seq = seq // tm
    in_specs = [pl.BlockSpec((tm, D_MODEL), lambda i: (i, 0)),
                pl.BlockSpec((tm, 1), lambda i: (i, 0))] + [_full_spec(w) for w in weights]
    out_shape = (
        jax.ShapeDtypeStruct((batch, N_HEADS, seq, HEAD_SLAB), MXU_DTYPE),
        jax.ShapeDtypeStruct((batch, N_HEADS, HEAD_SLAB, seq), MXU_DTYPE),
        jax.ShapeDtypeStruct((batch, seq, N_HEADS * V_HEAD), MXU_DTYPE),
        jax.ShapeDtypeStruct((t, D_MODEL), MXU_DTYPE),
        jax.ShapeDtypeStruct((t, D_MODEL), MXU_DTYPE),
    )
    out_specs = (
        pl.BlockSpec((1, N_HEADS, tm, HEAD_SLAB), lambda i: (i // tiles_per_seq, 0, i % tiles_per_seq, 0)),
        pl.BlockSpec((1, N_HEADS, HEAD_SLAB, tm), lambda i: (i // tiles_per_seq, 0, 0, i % tiles_per_seq)),
        pl.BlockSpec((1, tm, N_HEADS * V_HEAD), lambda i: (i // tiles_per_seq, i % tiles_per_seq, 0)),
        pl.BlockSpec((tm, D_MODEL), lambda i: (i, 0)),
        pl.BlockSpec((tm, D_MODEL), lambda i: (i, 0)),
    )
    return pl.pallas_call(
        functools.partial(_premix_kernel, tiles_per_seq),
        grid=(t // tm,),
        in_specs=in_specs,
        out_specs=out_specs,
        out_shape=out_shape,
        scratch_shapes=[pltpu.VMEM((POOL_HALO, POOL_WIDTH), jnp.float32)],
        compiler_params=pltpu.CompilerParams(
            dimension_semantics=("arbitrary",), vmem_limit_bytes=VMEM_LIMIT_BYTES),
        name="premix",
    )(x2d, pos2d, *weights)


def _attention_kernel(q_ref, kt_ref, v_ref, o_ref, va_ref, vb_ref, s_ref, m_ref, acc_ref):
    tq = q_ref.shape[2]
    tk = tq
    qi = pl.program_id(2)

    ones_lane = (V_HEAD, 0)

    @pl.when(qi == 0)
    def _():
        v = v_ref[0].astype(jnp.float32)
        lane = lax.broadcasted_iota(jnp.int32, v.shape, 1)
        pad_a = (lane == ones_lane[0]).astype(jnp.float32)
        pad_b = (lane == ones_lane[1]).astype(jnp.float32)
        va_ref[...] = jnp.where(lane < V_HEAD, v, pad_a).astype(va_ref.dtype)
        vb_ref[...] = jnp.where(lane >= V_HEAD, v, pad_b).astype(vb_ref.dtype)

    vsel = (va_ref, vb_ref)
    heads = range(2)

    def scores(blk, slot):
        start = pl.multiple_of(blk * tk, tk)
        for hh in heads:
            s_ref[slot, hh] = _dot(q_ref[0, hh], kt_ref[0, hh, :, pl.ds(start, tk)])

    def softmax_pv(blk, slot, masked):
        start = pl.multiple_of(blk * tk, tk)
        for hh in heads:
            if masked:
                row = lax.broadcasted_iota(jnp.int32, (tq, tk), 0)
                col = lax.broadcasted_iota(jnp.int32, (tq, tk), 1)
                s_ref[slot, hh] = jnp.where(row >= col, s_ref[slot, hh], NEG_BIG)
            m_old = m_ref[hh]
            m_new = jnp.maximum(m_old, jnp.max(s_ref[slot, hh], axis=-1, keepdims=True))
            p = jnp.exp2(s_ref[slot, hh] - m_new).astype(MXU_DTYPE)
            acc_ref[hh] = jnp.exp2(m_old - m_new) * acc_ref[hh] + _dot(p, vsel[hh][pl.ds(start, tk), :])
            m_ref[hh] = m_new

    m_ref[...] = jnp.full(m_ref.shape, NEG_BIG, jnp.float32)
    acc_ref[...] = jnp.zeros(acc_ref.shape, jnp.float32)
    scores(0, 0)

    def two_blocks(jj, carry):
        scores(2 * jj + 1, 1)
        softmax_pv(2 * jj, 0, False)
        scores(2 * jj + 2, 0)
        softmax_pv(2 * jj + 1, 1, False)
        return carry

    lax.fori_loop(0, qi // 2, two_blocks, 0)

    @pl.when(qi % 2 == 0)
    def _():
        softmax_pv(qi, 0, True)

    @pl.when(qi % 2 == 1)
    def _():
        scores(qi, 1)
        softmax_pv(qi - 1, 0, False)
        softmax_pv(qi, 1, True)

    lane = lax.broadcasted_iota(jnp.int32, (tq, HEAD_SLAB), 1)
    acc_a = acc_ref[0]
    acc_b = acc_ref[1]
    out_a = acc_a / acc_a[:, ones_lane[0]:ones_lane[0] + 1]
    out_b = acc_b / acc_b[:, ones_lane[1]:ones_lane[1] + 1]
    o_ref[0] = jnp.where(lane < V_HEAD, out_a, out_b).astype(o_ref.dtype)


def _attention(q, kt, v):
    batch, _, seq, _ = q.shape
    tq = ATTN_TILE
    pairs = N_HEADS // 2
    return pl.pallas_call(
        _attention_kernel,
        grid=(batch, pairs, seq // tq),
        in_specs=[
            pl.BlockSpec((1, 2, tq, HEAD_SLAB), lambda b, p, i: (b, p, i, 0)),
            pl.BlockSpec((1, 2, HEAD_SLAB, seq), lambda b, p, i: (b, p, 0, 0)),
            pl.BlockSpec((1, seq, 2 * V_HEAD), lambda b, p, i: (b, 0, p)),
        ],
        out_specs=pl.BlockSpec((1, tq, 2 * V_HEAD), lambda b, p, i: (b, i, p)),
        out_shape=jax.ShapeDtypeStruct((batch, seq, N_HEADS * V_HEAD), MXU_DTYPE),
        scratch_shapes=[pltpu.VMEM((seq, 2 * V_HEAD), MXU_DTYPE), pltpu.VMEM((seq, 2 * V_HEAD), MXU_DTYPE),
                        pltpu.VMEM((2, 2, tq, tq), jnp.float32),
                        pltpu.VMEM((2, tq, 1), jnp.float32),
                        pltpu.VMEM((2, tq, HEAD_SLAB), jnp.float32)],
        compiler_params=pltpu.CompilerParams(
            dimension_semantics=("arbitrary", "arbitrary", "arbitrary"), vmem_limit_bytes=VMEM_LIMIT_BYTES),
        name="attention",
    )(q, kt, v)


def _lane_pack(cols, lane):
    out = jnp.zeros(lane.shape, cols[0].dtype)
    for k, c in enumerate(cols):
        out = jnp.where(lane == k, c, out)
    return out


def _postmix_kernel(x_ref, at_ref, ga_ref, gp_ref, wao_ref, wo_ref, gffn_ref, wrh_ref, wrl_ref, br_ref,
                    x1_ref, h2_ref, idx_ref, rank_ref, gate_ref, cnt_ref, carry_ref):
    tm = x_ref.shape[0]

    @pl.when(pl.program_id(0) == 0)
    def _():
        carry_ref[...] = jnp.zeros_like(carry_ref)

    y_attn = _dot(at_ref[...], wao_ref[...])
    merged = ga_ref[...].astype(jnp.float32) * y_attn + gp_ref[...].astype(jnp.float32)
    x1 = x_ref[...] + _dot(merged.astype(MXU_DTYPE), wo_ref[...])
    x1_ref[...] = x1
    h2 = x1 * _rms_scale(x1, D_MODEL) * gffn_ref[...]
    h2_ref[...] = h2

    h_hi = h2.astype(MXU_DTYPE)
    h_lo = (h2 - h_hi.astype(jnp.float32)).astype(MXU_DTYPE)
    logits = _dot(h_hi, wrh_ref[...]) + (_dot(h_lo, wrh_ref[...]) + _dot(h_hi, wrl_ref[...])) + br_ref[...]

    lane = lax.broadcasted_iota(jnp.int32, (tm, LANES), 1)
    work = logits
    vals, idxs = [], []
    for _ in range(TOP_K):
        mx = jnp.max(work, axis=-1, keepdims=True)
        ix = jnp.min(jnp.where(work == mx, lane, LANES), axis=-1, keepdims=True)
        vals.append(mx)
        idxs.append(ix)
        work = jnp.where(lane == ix, -jnp.inf, work)
    exps = [jnp.exp(v - vals[0]) for v in vals]
    denom = exps[0] + exps[1] + exps[2] + exps[3]
    gates = [e / denom for e in exps]

    onehot = jnp.zeros((tm, LANES), jnp.float32)
    for ix in idxs:
        onehot = onehot + (lane == ix).astype(jnp.float32)
    r_i = lax.broadcasted_iota(jnp.int32, (tm, tm), 0)
    c_i = lax.broadcasted_iota(jnp.int32, (tm, tm), 1)
    lower = (c_i < r_i).astype(MXU_DTYPE)
    before = carry_ref[...] + _dot(lower, onehot.astype(MXU_DTYPE))
    ranks = [jnp.sum(jnp.where(lane == ix, before, 0.0), axis=-1, keepdims=True).astype(jnp.int32)
             for ix in idxs]
    carry_ref[...] = carry_ref[...] + jnp.sum(onehot, axis=0, keepdims=True)
    cnt_ref[...] = carry_ref[...].astype(jnp.int32)

    idx_ref[...] = _lane_pack(idxs, lane)[:, :TOP_K]
    rank_ref[...] = _lane_pack(ranks, lane)[:, :TOP_K]
    gate_ref[...] = _lane_pack(gates, lane)[:, :TOP_K]


def _postmix(x2d, attn2d, ga, gp, w_attn_out, w_o, g_ffn, w_router, b_router):
    t = x2d.shape[0]
    tm = TOKEN_TILE
    f32 = jnp.float32
    wr = jnp.zeros((D_MODEL, LANES), f32).at[:, :N_EXPERTS].set(w_router)
    wr_hi = wr.astype(MXU_DTYPE)
    wr_lo = (wr - wr_hi.astype(f32)).astype(MXU_DTYPE)
    br = jnp.full((1, LANES), NEG_BIG, f32).at[0, :N_EXPERTS].set(b_router)
    weights = (w_attn_out.astype(MXU_DTYPE), w_o.astype(MXU_DTYPE), g_ffn[None, :], wr_hi, wr_lo, br)
    row_spec = lambda width: pl.BlockSpec((tm, width), lambda i: (i, 0))
    return pl.pallas_call(
        _postmix_kernel,
        grid=(t // tm,),
        in_specs=[row_spec(D_MODEL), row_spec(N_HEADS * V_HEAD), row_spec(D_MODEL), row_spec(D_MODEL)]
        + [_full_spec(w) for w in weights],
        out_specs=(row_spec(D_MODEL), row_spec(D_MODEL), row_spec(TOP_K), row_spec(TOP_K), row_spec(TOP_K),
                   pl.BlockSpec((1, LANES), lambda i: (0, 0))),
        out_shape=(
            jax.ShapeDtypeStruct((t, D_MODEL), f32),
            jax.ShapeDtypeStruct((t, D_MODEL), f32),
            jax.ShapeDtypeStruct((t, TOP_K), jnp.int32),
            jax.ShapeDtypeStruct((t, TOP_K), jnp.int32),
            jax.ShapeDtypeStruct((t, TOP_K), f32),
            jax.ShapeDtypeStruct((1, LANES), jnp.int32),
        ),
        scratch_shapes=[pltpu.VMEM((1, LANES), f32)],
        compiler_params=pltpu.CompilerParams(
            dimension_semantics=("arbitrary",), vmem_limit_bytes=VMEM_LIMIT_BYTES),
        name="postmix",
    )(x2d, attn2d, ga, gp, *weights)


def _row_copy_wait(src_ref, dst_ref, sem):
    pltpu.make_async_copy(src_ref, dst_ref, sem).wait()


def _to_token_tiles(dst_ref, rows2d):
    rows = rows2d.shape[0]
    for c in range(ROW_CHUNKS):
        dst_ref[pl.ds(c, rows, stride=ROW_CHUNKS), :] = rows2d[:, c * LANES:(c + 1) * LANES]


def _from_token_tiles(src_ref):
    rows = src_ref.shape[0] // ROW_CHUNKS
    return jnp.concatenate([src_ref[pl.ds(c, rows, stride=ROW_CHUNKS), :] for c in range(ROW_CHUNKS)], axis=1)


def _token_tile(ref, index):
    return ref.at[pl.ds(pl.multiple_of(index * ROW_CHUNKS, ROW_CHUNKS), ROW_CHUNKS)]


def _dispatch_kernel(dest_ref, h2_ref, xs_ref, tiles_ref, sem):
    tm = h2_ref.shape[0]
    base = pl.program_id(0) * (tm * TOP_K)
    _to_token_tiles(tiles_ref, h2_ref[...])

    def issue(t, carry):
        for k in range(TOP_K):
            d = dest_ref[base + t * TOP_K + k]
            pltpu.make_async_copy(_token_tile(tiles_ref, t), _token_tile(xs_ref, d), sem).start(priority=k % 2)
        return carry

    lax.fori_loop(0, tm, issue, 0, unroll=8)
    for _ in range(TOP_K):
        _row_copy_wait(tiles_ref, xs_ref.at[pl.ds(0, tm * ROW_CHUNKS)], sem)


def _dispatch(dest_flat, h2, total_rows):
    t = h2.shape[0]
    tm = TOKEN_TILE
    return pl.pallas_call(
        _dispatch_kernel,
        grid_spec=pltpu.PrefetchScalarGridSpec(
            num_scalar_prefetch=1,
            grid=(t // tm,),
            in_specs=[pl.BlockSpec((tm, D_MODEL), lambda i, dest: (i, 0))],
            out_specs=pl.BlockSpec(memory_space=pl.ANY),
            scratch_shapes=[pltpu.VMEM((tm * ROW_CHUNKS, LANES), jnp.float32), pltpu.SemaphoreType.DMA(())],
        ),
        out_shape=jax.ShapeDtypeStruct((total_rows * ROW_CHUNKS, LANES), jnp.float32),
        compiler_params=pltpu.CompilerParams(
            dimension_semantics=("arbitrary",), vmem_limit_bytes=VMEM_LIMIT_BYTES, has_side_effects=True),
        name="dispatch",
    )(dest_flat, h2)


def _experts_kernel(blk_ref, bexp_ref, nval_ref, first_ref, xs_ref, wgu_ref, bgu_ref, wd_ref, bd_ref,
                    y_ref, wgu_bf, wd_bf):
    i = pl.program_id(0)
    rows = xs_ref.shape[0] // ROW_CHUNKS

    @pl.when(first_ref[i] == 1)
    def _():
        wgu_bf[...] = wgu_ref[0].astype(MXU_DTYPE)
        wd_bf[...] = wd_ref[0].astype(MXU_DTYPE)

    def ffn(n_rows):
        tiles = pl.ds(0, n_rows * ROW_CHUNKS)
        row = lax.broadcasted_iota(jnp.int32, (n_rows, 1), 0)
        x = jnp.where(row < nval_ref[i], _from_token_tiles(xs_ref.at[tiles]), 0.0).astype(MXU_DTYPE)
        gu = _dot(x, wgu_bf[...]) + bgu_ref[0]
        gate = jnp.minimum(gu[:, :D_EXPERT], SWIGLU_LIMIT)
        up = jnp.clip(gu[:, D_EXPERT:], -SWIGLU_LIMIT, SWIGLU_LIMIT)
        act = (up + 1.0) * gate * jax.nn.sigmoid(SWIGLU_ALPHA * gate)
        _to_token_tiles(y_ref.at[tiles], _dot(act.astype(MXU_DTYPE), wd_bf[...]) + bd_ref[0])

    @pl.when(nval_ref[i] > rows // 2)
    def _():
        ffn(rows)

    @pl.when((nval_ref[i] > 0) & (nval_ref[i] <= rows // 2))
    def _():
        ffn(rows // 2)


def _experts(blk, bexp, nval, first, xs, w_gate_up, b_gate_up, w_down, b_down):
    rows = EXPERT_ROWS
    nb = blk.shape[0]
    return pl.pallas_call(
        _experts_kernel,
        grid_spec=pltpu.PrefetchScalarGridSpec(
            num_scalar_prefetch=4,
            grid=(nb,),
            in_specs=[
                pl.BlockSpec((rows * ROW_CHUNKS, LANES), lambda i, blk, bexp, nval, first: (blk[i], 0)),
                pl.BlockSpec((1, D_MODEL, 2 * D_EXPERT), lambda i, blk, bexp, nval, first: (bexp[i], 0, 0)),
                pl.BlockSpec((1, 1, 2 * D_EXPERT), lambda i, blk, bexp, nval, first: (bexp[i], 0, 0)),
                pl.BlockSpec((1, D_EXPERT, D_MODEL), lambda i, blk, bexp, nval, first: (bexp[i], 0, 0)),
                pl.BlockSpec((1, 1, D_MODEL), lambda i, blk, bexp, nval, first: (bexp[i], 0, 0)),
            ],
            out_specs=pl.BlockSpec((rows * ROW_CHUNKS, LANES), lambda i, blk, bexp, nval, first: (blk[i], 0)),
            scratch_shapes=[pltpu.VMEM((D_MODEL, 2 * D_EXPERT), MXU_DTYPE),
                            pltpu.VMEM((D_EXPERT, D_MODEL), MXU_DTYPE)],
        ),
        out_shape=jax.ShapeDtypeStruct(xs.shape, jnp.float32),
        compiler_params=pltpu.CompilerParams(
            dimension_semantics=("arbitrary",), vmem_limit_bytes=VMEM_LIMIT_BYTES),
        name="experts",
    )(blk, bexp, nval, first, xs, w_gate_up, b_gate_up[:, None, :], w_down, b_down[:, None, :])


def _expert_blocks(counts, tokens):
    rows = EXPERT_ROWS
    nb_max = (tokens * TOP_K) // rows + N_EXPERTS
    nblk = (counts + rows - 1) // rows
    ends = jnp.cumsum(nblk)
    starts = ends - nblk
    total = ends[-1]
    step = jnp.arange(nb_max, dtype=jnp.int32)
    i = jnp.minimum(step, total - 1)[:, None]
    owner = ((i >= starts[None, :]) & (i < ends[None, :])).astype(jnp.int32)
    pick = lambda v: jnp.sum(owner * v[None, :], axis=1)
    e = pick(jnp.arange(N_EXPERTS, dtype=jnp.int32))
    j = i[:, 0] - pick(starts)
    active = step < total
    blk = e * (tokens // rows) + j
    nval = jnp.where(active, jnp.clip(pick(counts) - j * rows, 0, rows), 0)
    first = jnp.where(active & (j == 0), 1, 0)
    return blk.astype(jnp.int32), e.astype(jnp.int32), nval.astype(jnp.int32), first.astype(jnp.int32)


def _combine_kernel(dest_ref, gate_ref, x1_ref, p_ref, y_ref, gple_ref, wpg_ref, wple_ref,
                    o_ref, buf, sems):
    tm = x1_ref.shape[0]
    i = pl.program_id(0)
    n = pl.num_programs(0)

    def issue(tile, slot):
        base = tile * (tm * TOP_K)

        def body(t, carry):
            for k in range(TOP_K):
                d = dest_ref[base + t * TOP_K + k]
                pltpu.make_async_copy(_token_tile(y_ref, d), _token_tile(buf.at[slot, k], t),
                                      sems.at[slot]).start(priority=k % 2)
            return carry

        lax.fori_loop(0, tm, body, 0, unroll=8)

    @pl.when(i == 0)
    def _():
        issue(0, 0)

    @pl.when(i + 1 < n)
    def _():
        issue(i + 1, (i + 1) % 2)

    slot = i % 2
    for k in range(TOP_K):
        _row_copy_wait(y_ref.at[pl.ds(0, tm * ROW_CHUNKS)], buf.at[slot, k], sems.at[slot])

    gates = gate_ref[...]
    moe = gates[:, 0:1] * _from_token_tiles(buf.at[slot, 0])
    for k in range(1, TOP_K):
        moe = moe + gates[:, k:k + 1] * _from_token_tiles(buf.at[slot, k])
    x2 = x1_ref[...] + moe
    hp = (x2 * _rms_scale(x2, D_MODEL) * gple_ref[...]).astype(MXU_DTYPE)
    ple_gate = jax.nn.sigmoid(_dot(hp, wpg_ref[...]))
    o_ref[...] = x2 + ple_gate * _dot(p_ref[...].astype(MXU_DTYPE), wple_ref[...])


def _combine(dest_flat, gates, x1, p2d, y, g_ple, w_ple_gate, w_ple):
    t = x1.shape[0]
    tm = TOKEN_TILE
    weights = (g_ple[None, :], w_ple_gate.astype(MXU_DTYPE), w_ple.astype(MXU_DTYPE))
    return pl.pallas_call(
        _combine_kernel,
        grid_spec=pltpu.PrefetchScalarGridSpec(
            num_scalar_prefetch=1,
            grid=(t // tm,),
            in_specs=[
                pl.BlockSpec((tm, TOP_K), lambda i, dest: (i, 0)),
                pl.BlockSpec((tm, D_MODEL), lambda i, dest: (i, 0)),
                pl.BlockSpec((tm, PLE_DIM), lambda i, dest: (i, 0)),
                pl.BlockSpec(memory_space=pl.ANY),
            ] + [pl.BlockSpec(w.shape, lambda i, dest, _nd=w.ndim: (0,) * _nd) for w in weights],
            out_specs=pl.BlockSpec((tm, D_MODEL), lambda i, dest: (i, 0)),
            scratch_shapes=[pltpu.VMEM((2, TOP_K, tm * ROW_CHUNKS, LANES), jnp.float32),
                            pltpu.SemaphoreType.DMA((2,))],
        ),
        out_shape=jax.ShapeDtypeStruct((t, D_MODEL), jnp.float32),
        compiler_params=pltpu.CompilerParams(
            dimension_semantics=("arbitrary",), vmem_limit_bytes=VMEM_LIMIT_BYTES),
        name="combine",
    )(dest_flat, gates, x1, p2d, y, *weights)


def kernel(x, p, positions, g_mix, w_in, g_q_lat, w_q_b, g_kv_lat, w_kv_b, g_q_head, g_k_nope, g_k_rope, w_pool, pool_scale, w_attn_out, w_pool_out, w_o, g_ffn, w_router, b_router, w_gate_up, b_gate_up, w_down, b_down, g_ple, w_ple_gate, w_ple):
    batch, seq, _ = x.shape
    tokens = batch * seq
    depth = g_mix.shape[0]
    assert seq % TOKEN_TILE == 0 and seq % ATTN_TILE == 0 and tokens % EXPERT_ROWS == 0
    x2d = x.reshape(tokens, D_MODEL)
    pos2d = positions.astype(jnp.float32).reshape(tokens, 1)
    for i in range(depth):
        weights = _premix_weights(g_mix[i], w_in[i], g_q_lat[i], w_q_b[i], g_kv_lat[i], w_kv_b[i],
                                  g_q_head[i], g_k_nope[i], g_k_rope[i], w_pool[i], pool_scale[i],
                                  w_pool_out[i])
        q, kt, v, ga, gp = _premix(x2d, pos2d, weights, batch, seq)
        attn = _attention(q, kt, v).reshape(tokens, N_HEADS * V_HEAD)
        x1, h2, idx, rank, gates, counts = _postmix(x2d, attn, ga, gp, w_attn_out[i], w_o[i], g_ffn[i],
                                                    w_router[i], b_router[i])
        dest = (idx * tokens + rank).reshape(-1)
        blk, bexp, nval, first = _expert_blocks(counts[0, :N_EXPERTS], tokens)
        xs = _dispatch(dest, h2, N_EXPERTS * tokens)
        y = _experts(blk, bexp, nval, first, xs, w_gate_up[i], b_gate_up[i], w_down[i], b_down[i])
        x2d = _combine(dest, gates, x1, p[i].reshape(tokens, PLE_DIM), y, g_ple[i], w_ple_gate[i], w_ple[i])
    return x2d.reshape(batch, seq, D_MODEL)
```

```python
import functools
import math

import jax
import jax.numpy as jnp
import numpy as np
from jax import lax
from jax.experimental import pallas as pl
from jax.experimental.pallas import tpu as pltpu

D_MODEL = 1024
N_HEADS = 8
QK_NOPE = 64
QK_ROPE = 32
QK_HEAD = QK_NOPE + QK_ROPE
V_HEAD = 64
Q_LORA = 256
KV_LORA = 128
ROPE_THETA = 10000.0
EPS = 1e-6
POOL_WINDOWS = (2, 4, 8, 16)
POOL_GROUPS = 4
POOL_WIDTH = 512
POOL_GROUP_DIM = POOL_WIDTH // POOL_GROUPS
POOL_HALO = 16
N_EXPERTS = 32
TOP_K = 4
D_EXPERT = 1024
SWIGLU_ALPHA = 1.702
SWIGLU_LIMIT = 7.0
PLE_DIM = 256

LANES = 128
HEAD_SLAB = LANES
ROW_CHUNKS = D_MODEL // LANES
VMEM_LIMIT_BYTES = 56 * 1024 * 1024

MXU_DTYPE = jnp.bfloat16
NEG_BIG = -1e30
Q_SCALE = math.log2(math.e) / math.sqrt(QK_HEAD)

TOKEN_TILE = 512
ATTN_TILE = 512
EXPERT_ROWS = 512


def _dot(a, b):
    return jnp.dot(a, b, preferred_element_type=jnp.float32)


def _rms_scale(v, width):
    return lax.rsqrt(jnp.sum(v * v, axis=-1, keepdims=True) * (1.0 / width) + EPS)


_C_QLAT = 0
_C_KVLAT = _C_QLAT + Q_LORA
_C_KR = _C_KVLAT + KV_LORA
_C_KRR = _C_KR + HEAD_SLAB
_C_POOL = _C_KRR + HEAD_SLAB
_C_GA = _C_POOL + POOL_WIDTH
_C_GP = _C_GA + D_MODEL
_C_END = _C_GP + D_MODEL


def _premix_kernel(tiles_per_seq, x_ref, pos_ref, gmix_ref, w1_ref, gql_ref, wqs_ref, wqr_ref,
                   gkvl_ref, wks_ref, wv_ref, gqs_ref, gqr_ref, gks_ref, gkrs_ref, gkrr_ref,
                   invf_ref, wpool_ref, pscale_ref, wpo_ref,
                   q_ref, kt_ref, v_ref, ga_ref, gp_ref, halo_ref, proj_ref):
    @pl.when(pl.program_id(0) == 0)
    def _():
        halo_ref[...] = jnp.zeros_like(halo_ref)

    xv = x_ref[...]
    h = (xv * _rms_scale(xv, D_MODEL) * gmix_ref[...]).astype(MXU_DTYPE)
    proj_ref[...] = _dot(h, w1_ref[...])
    for section in _premix_sections(tiles_per_seq, proj_ref, pos_ref, gql_ref, wqs_ref, wqr_ref, gkvl_ref,
                                    wks_ref, wv_ref, gqs_ref, gqr_ref, gks_ref, gkrs_ref, gkrr_ref, invf_ref,
                                    wpool_ref, pscale_ref, wpo_ref, q_ref, kt_ref, v_ref, ga_ref, gp_ref,
                                    halo_ref):
        section()


def _premix_sections(tiles_per_seq, proj, pos_ref, gql_ref, wqs_ref, wqr_ref, gkvl_ref, wks_ref, wv_ref,
                     gqs_ref, gqr_ref, gks_ref, gkrs_ref, gkrr_ref, invf_ref, wpool_ref, pscale_ref,
                     wpo_ref, q_ref, kt_ref, v_ref, ga_ref, gp_ref, halo_ref):
    tm = proj.shape[0]
    si = pl.program_id(0) % tiles_per_seq
    shared = {}

    def rope_tables():
        if "cos" not in shared:
            ang = pos_ref[...] * invf_ref[...]
            shared["cos"] = jnp.cos(ang)
            shared["sin"] = jnp.sin(ang)
        return shared["cos"], shared["sin"]

    def q_project():
        q_lat = proj[:, _C_QLAT:_C_QLAT + Q_LORA]
        qln = (q_lat * _rms_scale(q_lat, Q_LORA) * gql_ref[...]).astype(MXU_DTYPE)
        shared["qs"] = _dot(qln, wqs_ref[...])
        shared["qr"] = _dot(qln, wqr_ref[...])

    def q_heads(heads):
        def run():
            cos_t, sin_t = rope_tables()
            lane = lax.broadcasted_iota(jnp.int32, (tm, LANES), 1)
            q_cos = gqs_ref[...] * jnp.where(lane < QK_NOPE, 1.0, cos_t)
            q_sin = gqr_ref[...] * sin_t
            for hd in heads:
                s = shared["qs"][:, hd * HEAD_SLAB:(hd + 1) * HEAD_SLAB]
                r = shared["qr"][:, hd * HEAD_SLAB:(hd + 1) * HEAD_SLAB]
                scale = _rms_scale(s, QK_HEAD) * Q_SCALE
                q_ref[0, hd] = ((s * q_cos + r * q_sin) * scale).astype(q_ref.dtype)
        return run

    def kv_project():
        cos_t, sin_t = rope_tables()
        kv_lat = proj[:, _C_KVLAT:_C_KVLAT + KV_LORA]
        kvn = (kv_lat * _rms_scale(kv_lat, KV_LORA) * gkvl_ref[...]).astype(MXU_DTYPE)
        shared["ks"] = _dot(kvn, wks_ref[...])
        v_ref[0] = _dot(kvn, wv_ref[...]).astype(v_ref.dtype)
        krs = proj[:, _C_KR:_C_KR + HEAD_SLAB]
        krr = proj[:, _C_KRR:_C_KRR + HEAD_SLAB]
        shared["k_rot"] = ((krs * (gkrs_ref[...] * cos_t) + krr * (gkrr_ref[...] * sin_t))
                           * _rms_scale(krs, QK_ROPE))

    def k_heads(heads):
        def run():
            for hd in heads:
                s = shared["ks"][:, hd * HEAD_SLAB:(hd + 1) * HEAD_SLAB]
                kh = s * _rms_scale(s, QK_NOPE) * gks_ref[...] + shared["k_rot"]
                kt_ref[0, hd] = kh.T.astype(kt_ref.dtype)
        return run

    def pool_groups(groups):
        def run():
            t_seq = si * tm + lax.broadcasted_iota(jnp.int32, (tm, 1), 0)
            for g in groups:
                w = POOL_WINDOWS[g]
                cols = slice(g * POOL_GROUP_DIM, (g + 1) * POOL_GROUP_DIM)
                u = proj[:, _C_POOL + cols.start:_C_POOL + cols.stop]
                history = jnp.where(si == 0, 0.0, halo_ref[:, cols])
                acc = jnp.concatenate([history, u], axis=0)
                halo_ref[:, cols] = u[tm - POOL_HALO:, :]
                shift = 1
                while shift < w:
                    acc = acc + pltpu.roll(acc, shift, axis=0)
                    shift *= 2
                count = jnp.minimum(t_seq + 1, w).astype(jnp.float32)
                pooled = acc[POOL_HALO:, :] / count - u
                shared[("mixed", g)] = _dot(pooled.astype(MXU_DTYPE), wpool_ref[g])
        return run

    def pool_out():
        mixed = jnp.concatenate([shared[("mixed", g)] for g in range(POOL_GROUPS)], axis=1) * pscale_ref[...]
        shared["y_pool"] = _dot(mixed.astype(MXU_DTYPE), wpo_ref[...])

    def gate_attn():
        ga_ref[...] = jax.nn.sigmoid(proj[:, _C_GA:_C_GA + D_MODEL]).astype(ga_ref.dtype)

    def gate_pool():
        gp_ref[...] = (jax.nn.sigmoid(proj[:, _C_GP:_C_GP + D_MODEL]) * shared["y_pool"]).astype(gp_ref.dtype)

    half = N_HEADS // 2
    return [q_project, kv_project, q_heads(range(half)), q_heads(range(half, N_HEADS)),
            k_heads(range(half)), k_heads(range(half, N_HEADS)),
            pool_groups((0, 1)), pool_groups((2, 3)), pool_out, gate_attn, gate_pool]


def _slab_cols(width_per_head):
    return (np.arange(N_HEADS)[:, None] * HEAD_SLAB + np.arange(width_per_head)[None, :]).reshape(-1)


def _premix_weights(g_mix, w_in, g_q_lat, w_q_b, g_kv_lat, w_kv_b, g_q_head, g_k_nope, g_k_rope,
                    w_pool, pool_scale, w_pool_out):
    f32 = jnp.float32
    half = QK_ROPE // 2
    w_q_lat = w_in[:, :Q_LORA]
    w_kv_lat = w_in[:, Q_LORA:Q_LORA + KV_LORA]
    w_kr = w_in[:, Q_LORA + KV_LORA:Q_LORA + KV_LORA + QK_ROPE]
    off = Q_LORA + KV_LORA + QK_ROPE
    w_pool_in = w_in[:, off:off + POOL_WIDTH]
    w_ga = w_in[:, off + POOL_WIDTH:off + POOL_WIDTH + D_MODEL]
    w_gp = w_in[:, off + POOL_WIDTH + D_MODEL:]

    kr_s = jnp.zeros((D_MODEL, HEAD_SLAB), f32).at[:, QK_NOPE:QK_NOPE + QK_ROPE].set(w_kr)
    kr_r = jnp.zeros((D_MODEL, HEAD_SLAB), f32)
    kr_r = kr_r.at[:, QK_NOPE:QK_NOPE + half].set(-w_kr[:, half:])
    kr_r = kr_r.at[:, QK_NOPE + half:QK_NOPE + QK_ROPE].set(w_kr[:, :half])
    w1 = jnp.concatenate([w_q_lat, w_kv_lat, kr_s, kr_r, w_pool_in, w_ga, w_gp], axis=1)

    wq = w_q_b.reshape(Q_LORA, N_HEADS, QK_HEAD)
    wq_s = jnp.zeros((Q_LORA, N_HEADS, HEAD_SLAB), f32).at[:, :, :QK_HEAD].set(wq)
    wq_r = jnp.zeros((Q_LORA, N_HEADS, HEAD_SLAB), f32)
    wq_r = wq_r.at[:, :, QK_NOPE:QK_NOPE + half].set(-wq[:, :, QK_NOPE + half:])
    wq_r = wq_r.at[:, :, QK_NOPE + half:QK_HEAD].set(wq[:, :, QK_NOPE:QK_NOPE + half])
    wkv = w_kv_b.reshape(KV_LORA, N_HEADS, QK_NOPE + V_HEAD)
    wk_s = jnp.zeros((KV_LORA, N_HEADS, HEAD_SLAB), f32).at[:, :, :QK_NOPE].set(wkv[:, :, :QK_NOPE])
    wv = wkv[:, :, QK_NOPE:].reshape(KV_LORA, N_HEADS * V_HEAD)

    def slab(vals, start):
        return jnp.zeros((1, HEAD_SLAB), f32).at[0, start:start + vals.shape[0]].set(vals)

    gq_s = slab(g_q_head, 0)
    gq_r = slab(jnp.concatenate([g_q_head[QK_NOPE + half:], g_q_head[QK_NOPE:QK_NOPE + half]]), QK_NOPE)
    gk_s = slab(g_k_nope, 0)
    gkr_s = slab(g_k_rope, QK_NOPE)
    gkr_r = slab(jnp.concatenate([g_k_rope[half:], g_k_rope[:half]]), QK_NOPE)
    inv_freq = ROPE_THETA ** (-jnp.arange(0, QK_ROPE, 2, dtype=f32) / QK_ROPE)
    invf = jnp.tile(inv_freq, LANES // half)[None, :]
    bf = MXU_DTYPE
    return (g_mix[None, :], w1.astype(bf), g_q_lat[None, :],
            wq_s.reshape(Q_LORA, -1).astype(bf), wq_r.reshape(Q_LORA, -1).astype(bf),
            g_kv_lat[None, :], wk_s.reshape(KV_LORA, -1).astype(bf), wv.astype(bf),
            gq_s, gq_r, gk_s, gkr_s, gkr_r, invf, w_pool.astype(bf), pool_scale[None, :],
            w_pool_out.astype(bf))


def _full_spec(arr):
    nd = arr.ndim
    return pl.BlockSpec(arr.shape, lambda i, _nd=nd: (0,) * _nd)


def _premix(x2d, pos2d, weights, batch, seq):
    t = x2d.shape[0]
    tm = TOKEN_TILE
    tiles_per_seq = seq // tm
    in_specs = [pl.BlockSpec((tm, D_MODEL), lambda i: (i, 0)),
                pl.BlockSpec((tm, 1), lambda i: (i, 0))] + [_full_spec(w) for w in weights]
    out_shape = (
        jax.ShapeDtypeStruct((batch, N_HEADS, seq, HEAD_SLAB), MXU_DTYPE),
        jax.ShapeDtypeStruct((batch, N_HEADS, HEAD_SLAB, seq), MXU_DTYPE),
        jax.ShapeDtypeStruct((batch, seq, N_HEADS * V_HEAD), MXU_DTYPE),
        jax.ShapeDtypeStruct((t, D_MODEL), MXU_DTYPE),
        jax.ShapeDtypeStruct((t, D_MODEL), MXU_DTYPE),
    )
    out_specs = (
        pl.BlockSpec((1, N_HEADS, tm, HEAD_SLAB), lambda i: (i // tiles_per_seq, 0, i % tiles_per_seq, 0)),
        pl.BlockSpec((1, N_HEADS, HEAD_SLAB, tm), lambda i: (i // tiles_per_seq, 0, 0, i % tiles_per_seq)),
        pl.BlockSpec((1, tm, N_HEADS * V_HEAD), lambda i: (i // tiles_per_seq, i % tiles_per_seq, 0)),
        pl.BlockSpec((tm, D_MODEL), lambda i: (i, 0)),
        pl.BlockSpec((tm, D_MODEL), lambda i: (i, 0)),
    )
    return pl.pallas_call(
        functools.partial(_premix_kernel, tiles_per_seq),
        grid=(t // tm,),
        in_specs=in_specs,
        out_specs=out_specs,
        out_shape=out_shape,
        scratch_shapes=[pltpu.VMEM((POOL_HALO, POOL_WIDTH), jnp.float32),
                        pltpu.VMEM((tm, _C_END), jnp.float32)],
        compiler_params=pltpu.CompilerParams(
            dimension_semantics=("arbitrary",), vmem_limit_bytes=VMEM_LIMIT_BYTES),
        name="premix",
    )(x2d, pos2d, *weights)


def _attention_kernel(q_ref, kt_ref, v_ref, o_ref, va_ref, vb_ref, s_ref, m_ref, acc_ref):
    tq = q_ref.shape[2]
    tk = tq
    qi = pl.program_id(2)

    ones_lane = (V_HEAD, 0)

    @pl.when(qi == 0)
    def _():
        v = v_ref[0].astype(jnp.float32)
        lane = lax.broadcasted_iota(jnp.int32, v.shape, 1)
        pad_a = (lane == ones_lane[0]).astype(jnp.float32)
        pad_b = (lane == ones_lane[1]).astype(jnp.float32)
        va_ref[...] = jnp.where(lane < V_HEAD, v, pad_a).astype(va_ref.dtype)
        vb_ref[...] = jnp.where(lane >= V_HEAD, v, pad_b).astype(vb_ref.dtype)

    vsel = (va_ref, vb_ref)
    heads = range(2)

    def scores(blk, slot):
        start = pl.multiple_of(blk * tk, tk)
        for hh in heads:
            s_ref[slot, hh] = _dot(q_ref[0, hh], kt_ref[0, hh, :, pl.ds(start, tk)])

    def softmax_pv(blk, slot, masked):
        start = pl.multiple_of(blk * tk, tk)
        for hh in heads:
            if masked:
                row = lax.broadcasted_iota(jnp.int32, (tq, tk), 0)
                col = lax.broadcasted_iota(jnp.int32, (tq, tk), 1)
                s_ref[slot, hh] = jnp.where(row >= col, s_ref[slot, hh], NEG_BIG)
            m_old = m_ref[hh]
            m_new = jnp.maximum(m_old, jnp.max(s_ref[slot, hh], axis=-1, keepdims=True))
            p = jnp.exp2(s_ref[slot, hh] - m_new).astype(MXU_DTYPE)
            acc_ref[hh] = jnp.exp2(m_old - m_new) * acc_ref[hh] + _dot(p, vsel[hh][pl.ds(start, tk), :])
            m_ref[hh] = m_new

    m_ref[...] = jnp.full(m_ref.shape, NEG_BIG, jnp.float32)
    acc_ref[...] = jnp.zeros(acc_ref.shape, jnp.float32)
    scores(0, 0)

    def two_blocks(jj, carry):
        scores(2 * jj + 1, 1)
        softmax_pv(2 * jj, 0, False)
        scores(2 * jj + 2, 0)
        softmax_pv(2 * jj + 1, 1, False)
        return carry

    lax.fori_loop(0, qi // 2, two_blocks, 0)

    @pl.when(qi % 2 == 0)
    def _():
        softmax_pv(qi, 0, True)

    @pl.when(qi % 2 == 1)
    def _():
        scores(qi, 1)
        softmax_pv(qi - 1, 0, False)
        softmax_pv(qi, 1, True)

    lane = lax.broadcasted_iota(jnp.int32, (tq, HEAD_SLAB), 1)
    acc_a = acc_ref[0]
    acc_b = acc_ref[1]
    out_a = acc_a / acc_a[:, ones_lane[0]:ones_lane[0] + 1]
    out_b = acc_b / acc_b[:, ones_lane[1]:ones_lane[1] + 1]
    o_ref[0] = jnp.where(lane < V_HEAD, out_a, out_b).astype(o_ref.dtype)


def _attention(q, kt, v):
    batch, _, seq, _ = q.shape
    tq = ATTN_TILE
    pairs = N_HEADS // 2
    return pl.pallas_call(
        _attention_kernel,
        grid=(batch, pairs, seq // tq),
        in_specs=[
            pl.BlockSpec((1, 2, tq, HEAD_SLAB), lambda b, p, i: (b, p, i, 0)),
            pl.BlockSpec((1, 2, HEAD_SLAB, seq), lambda b, p, i: (b, p, 0, 0)),
            pl.BlockSpec((1, seq, 2 * V_HEAD), lambda b, p, i: (b, 0, p)),
        ],
        out_specs=pl.BlockSpec((1, tq, 2 * V_HEAD), lambda b, p, i: (b, i, p)),
        out_shape=jax.ShapeDtypeStruct((batch, seq, N_HEADS * V_HEAD), MXU_DTYPE),
        scratch_shapes=[pltpu.VMEM((seq, 2 * V_HEAD), MXU_DTYPE), pltpu.VMEM((seq, 2 * V_HEAD), MXU_DTYPE),
                        pltpu.VMEM((2, 2, tq, tq), jnp.float32),
                        pltpu.VMEM((2, tq, 1), jnp.float32),
                        pltpu.VMEM((2, tq, HEAD_SLAB), jnp.float32)],
        compiler_params=pltpu.CompilerParams(
            dimension_semantics=("arbitrary", "arbitrary", "arbitrary"), vmem_limit_bytes=VMEM_LIMIT_BYTES),
        name="attention",
    )(q, kt, v)


def _lane_pack(cols, lane):
    out = jnp.zeros(lane.shape, cols[0].dtype)
    for k, c in enumerate(cols):
        out = jnp.where(lane == k, c, out)
    return out


def _postmix_kernel(x_ref, at_ref, ga_ref, gp_ref, wao_ref, wo_ref, gffn_ref, wrh_ref, wrl_ref, br_ref,
                    x1_ref, h2_ref, idx_ref, rank_ref, gate_ref, cnt_ref, carry_ref, lower_ref):
    tm = x_ref.shape[0]

    @pl.when(pl.program_id(0) == 0)
    def _():
        carry_ref[...] = jnp.zeros_like(carry_ref)
        r_i = lax.broadcasted_iota(jnp.int32, (tm, tm), 0)
        c_i = lax.broadcasted_iota(jnp.int32, (tm, tm), 1)
        lower_ref[...] = (c_i < r_i).astype(lower_ref.dtype)

    y_attn = _dot(at_ref[...], wao_ref[...])
    merged = ga_ref[...].astype(jnp.float32) * y_attn + gp_ref[...].astype(jnp.float32)
    x1_ref[...] = x_ref[...] + _dot(merged.astype(MXU_DTYPE), wo_ref[...])
    x1 = x1_ref[...]
    h2_ref[...] = x1 * _rms_scale(x1, D_MODEL) * gffn_ref[...]
    h2 = h2_ref[...]

    h_hi = h2.astype(MXU_DTYPE)
    h_lo = (h2 - h_hi.astype(jnp.float32)).astype(MXU_DTYPE)
    logits = _dot(h_hi, wrh_ref[...]) + (_dot(h_lo, wrh_ref[...]) + _dot(h_hi, wrl_ref[...])) + br_ref[...]

    lane = lax.broadcasted_iota(jnp.int32, (tm, LANES), 1)
    work = logits
    vals, idxs = [], []
    for _ in range(TOP_K):
        mx = jnp.max(work, axis=-1, keepdims=True)
        ix = jnp.min(jnp.where(work == mx, lane, LANES), axis=-1, keepdims=True)
        vals.append(mx)
        idxs.append(ix)
        work = jnp.where(lane == ix, -jnp.inf, work)
    exps = [jnp.exp(v - vals[0]) for v in vals]
    denom = exps[0] + exps[1] + exps[2] + exps[3]
    gates = [e / denom for e in exps]

    onehot = jnp.zeros((tm, LANES), jnp.float32)
    for ix in idxs:
        onehot = onehot + (lane == ix).astype(jnp.float32)
    before = carry_ref[...] + _dot(lower_ref[...], onehot.astype(MXU_DTYPE))
    ranks = [jnp.sum(jnp.where(lane == ix, before, 0.0), axis=-1, keepdims=True).astype(jnp.int32)
             for ix in idxs]
    carry_ref[...] = carry_ref[...] + jnp.sum(onehot, axis=0, keepdims=True)
    cnt_ref[...] = carry_ref[...].astype(jnp.int32)

    idx_ref[...] = _lane_pack(idxs, lane)[:, :TOP_K]
    rank_ref[...] = _lane_pack(ranks, lane)[:, :TOP_K]
    gate_ref[...] = _lane_pack(gates, lane)[:, :TOP_K]


def _postmix(x2d, attn2d, ga, gp, w_attn_out, w_o, g_ffn, w_router, b_router):
    t = x2d.shape[0]
    tm = TOKEN_TILE
    f32 = jnp.float32
    wr = jnp.zeros((D_MODEL, LANES), f32).at[:, :N_EXPERTS].set(w_router)
    wr_hi = wr.astype(MXU_DTYPE)
    wr_lo = (wr - wr_hi.astype(f32)).astype(MXU_DTYPE)
    br = jnp.full((1, LANES), NEG_BIG, f32).at[0, :N_EXPERTS].set(b_router)
    weights = (w_attn_out.astype(MXU_DTYPE), w_o.astype(MXU_DTYPE), g_ffn[None, :], wr_hi, wr_lo, br)
    row_spec = lambda width: pl.BlockSpec((tm, width), lambda i: (i, 0))
    return pl.pallas_call(
        _postmix_kernel,
        grid=(t // tm,),
        in_specs=[row_spec(D_MODEL), row_spec(N_HEADS * V_HEAD), row_spec(D_MODEL), row_spec(D_MODEL)]
        + [_full_spec(w) for w in weights],
        out_specs=(row_spec(D_MODEL), row_spec(D_MODEL), row_spec(TOP_K), row_spec(TOP_K), row_spec(TOP_K),
                   pl.BlockSpec((1, LANES), lambda i: (0, 0))),
        out_shape=(
            jax.ShapeDtypeStruct((t, D_MODEL), f32),
            jax.ShapeDtypeStruct((t, D_MODEL), f32),
            jax.ShapeDtypeStruct((t, TOP_K), jnp.int32),
            jax.ShapeDtypeStruct((t, TOP_K), jnp.int32),
            jax.ShapeDtypeStruct((t, TOP_K), f32),
            jax.ShapeDtypeStruct((1, LANES), jnp.int32),
        ),
        scratch_shapes=[pltpu.VMEM((1, LANES), f32),
                        pltpu.VMEM((tm, tm), MXU_DTYPE)],
        compiler_params=pltpu.CompilerParams(
            dimension_semantics=("arbitrary",), vmem_limit_bytes=VMEM_LIMIT_BYTES),
        name="postmix",
    )(x2d, attn2d, ga, gp, *weights)


def _row_copy_wait(src_ref, dst_ref, sem):
    pltpu.make_async_copy(src_ref, dst_ref, sem).wait()


def _to_token_tiles(dst_ref, rows2d):
    rows = rows2d.shape[0]
    for c in range(ROW_CHUNKS):
        dst_ref[pl.ds(c, rows, stride=ROW_CHUNKS), :] = rows2d[:, c * LANES:(c + 1) * LANES]


def _from_token_tiles(src_ref):
    rows = src_ref.shape[0] // ROW_CHUNKS
    return jnp.concatenate([src_ref[pl.ds(c, rows, stride=ROW_CHUNKS), :] for c in range(ROW_CHUNKS)], axis=1)


def _token_tile(ref, index):
    return ref.at[pl.ds(pl.multiple_of(index * ROW_CHUNKS, ROW_CHUNKS), ROW_CHUNKS)]


def _dispatch_kernel(dest_ref, h2_ref, xs_ref, tiles_ref, sem):
    tm = h2_ref.shape[0]
    base = pl.program_id(0) * (tm * TOP_K)
    _to_token_tiles(tiles_ref, h2_ref[...])

    def issue(t, carry):
        for k in range(TOP_K):
            d = dest_ref[base + t * TOP_K + k]
            pltpu.make_async_copy(_token_tile(tiles_ref, t), _token_tile(xs_ref, d), sem).start(priority=k % 2)
        return carry

    lax.fori_loop(0, tm, issue, 0, unroll=8)
    for _ in range(TOP_K):
        _row_copy_wait(tiles_ref, xs_ref.at[pl.ds(0, tm * ROW_CHUNKS)], sem)


def _dispatch(dest_flat, h2, total_rows):
    t = h2.shape[0]
    tm = TOKEN_TILE
    return pl.pallas_call(
        _dispatch_kernel,
        grid_spec=pltpu.PrefetchScalarGridSpec(
            num_scalar_prefetch=1,
            grid=(t // tm,),
            in_specs=[pl.BlockSpec((tm, D_MODEL), lambda i, dest: (i, 0))],
            out_specs=pl.BlockSpec(memory_space=pl.ANY),
            scratch_shapes=[pltpu.VMEM((tm * ROW_CHUNKS, LANES), jnp.float32), pltpu.SemaphoreType.DMA(())],
        ),
        out_shape=jax.ShapeDtypeStruct((total_rows * ROW_CHUNKS, LANES), jnp.float32),
        compiler_params=pltpu.CompilerParams(
            dimension_semantics=("arbitrary",), vmem_limit_bytes=VMEM_LIMIT_BYTES, has_side_effects=True),
        name="dispatch",
    )(dest_flat, h2)


def _experts_kernel(blk_ref, bexp_ref, nval_ref, first_ref, next_ref, slot_ref,
                    xs_ref, wgu_ref, bgu_ref, wd_ref, bd_ref,
                    y_ref, wgu_f32, wd_f32, wgu_bf, wd_bf, sems):
    i = pl.program_id(0)
    rows = xs_ref.shape[0] // ROW_CHUNKS

    def weight_copies(expert, slot):
        return (pltpu.make_async_copy(wgu_ref.at[expert], wgu_f32.at[slot], sems.at[slot, 0]),
                pltpu.make_async_copy(wd_ref.at[expert], wd_f32.at[slot], sems.at[slot, 1]))

    @pl.when(i == 0)
    def _():
        for copy in weight_copies(bexp_ref[0], 0):
            copy.start()

    @pl.when(first_ref[i] == 1)
    def _():
        slot = slot_ref[i]

        @pl.when(next_ref[i] >= 0)
        def _():
            for copy in weight_copies(next_ref[i], 1 - slot):
                copy.start()

        for copy in weight_copies(bexp_ref[i], slot):
            copy.wait()
        wgu_bf[...] = wgu_f32[slot].astype(MXU_DTYPE)
        wd_bf[...] = wd_f32[slot].astype(MXU_DTYPE)

    def ffn(n_rows):
        tiles = pl.ds(0, n_rows * ROW_CHUNKS)
        row = lax.broadcasted_iota(jnp.int32, (n_rows, 1), 0)
        x = jnp.where(row < nval_ref[i], _from_token_tiles(xs_ref.at[tiles]), 0.0).astype(MXU_DTYPE)
        gu = _dot(x, wgu_bf[...]) + bgu_ref[0]
        gate = jnp.minimum(gu[:, :D_EXPERT], SWIGLU_LIMIT)
        up = jnp.clip(gu[:, D_EXPERT:], -SWIGLU_LIMIT, SWIGLU_LIMIT)
        act = (up + 1.0) * gate * jax.nn.sigmoid(SWIGLU_ALPHA * gate)
        _to_token_tiles(y_ref.at[tiles], _dot(act.astype(MXU_DTYPE), wd_bf[...]) + bd_ref[0])

    @pl.when(nval_ref[i] > rows // 2)
    def _():
        ffn(rows)

    @pl.when((nval_ref[i] > 0) & (nval_ref[i] <= rows // 2))
    def _():
        ffn(rows // 2)


def _experts(table, xs, w_gate_up, b_gate_up, w_down, b_down):
    rows = EXPERT_ROWS
    nb = table[0].shape[0]
    row_block = pl.BlockSpec((rows * ROW_CHUNKS, LANES), lambda i, blk, *_: (blk[i], 0))
    return pl.pallas_call(
        _experts_kernel,
        grid_spec=pltpu.PrefetchScalarGridSpec(
            num_scalar_prefetch=len(table),
            grid=(nb,),
            in_specs=[
                row_block,
                pl.BlockSpec(memory_space=pl.ANY),
                pl.BlockSpec((1, 1, 2 * D_EXPERT), lambda i, blk, bexp, *_: (bexp[i], 0, 0)),
                pl.BlockSpec(memory_space=pl.ANY),
                pl.BlockSpec((1, 1, D_MODEL), lambda i, blk, bexp, *_: (bexp[i], 0, 0)),
            ],
            out_specs=row_block,
            scratch_shapes=[pltpu.VMEM((2, D_MODEL, 2 * D_EXPERT), jnp.float32),
                            pltpu.VMEM((2, D_EXPERT, D_MODEL), jnp.float32),
                            pltpu.VMEM((D_MODEL, 2 * D_EXPERT), MXU_DTYPE),
                            pltpu.VMEM((D_EXPERT, D_MODEL), MXU_DTYPE),
                            pltpu.SemaphoreType.DMA((2, 2))],
        ),
        out_shape=jax.ShapeDtypeStruct(xs.shape, jnp.float32),
        compiler_params=pltpu.CompilerParams(
            dimension_semantics=("arbitrary",), vmem_limit_bytes=VMEM_LIMIT_BYTES),
        name="experts",
    )(*table, xs, w_gate_up, b_gate_up[:, None, :], w_down, b_down[:, None, :])


def _expert_blocks(counts, tokens):
    rows = EXPERT_ROWS
    nb_max = (tokens * TOP_K) // rows + N_EXPERTS
    nblk = (counts + rows - 1) // rows
    ends = jnp.cumsum(nblk)
    starts = ends - nblk
    total = ends[-1]
    step = jnp.arange(nb_max, dtype=jnp.int32)
    i = jnp.minimum(step, total - 1)[:, None]
    owner = ((i >= starts[None, :]) & (i < ends[None, :])).astype(jnp.int32)
    pick = lambda v: jnp.sum(owner * v[None, :], axis=1)
    e = pick(jnp.arange(N_EXPERTS, dtype=jnp.int32))
    j = i[:, 0] - pick(starts)
    active = step < total
    blk = e * (tokens // rows) + j
    nval = jnp.where(active, jnp.clip(pick(counts) - j * rows, 0, rows), 0)
    first = jnp.where(active & (j == 0), 1, 0)
    ids = jnp.arange(N_EXPERTS, dtype=jnp.int32)
    has_rows = nblk > 0
    later = has_rows[None, :] & (ids[None, :] > ids[:, None])
    nxt = jnp.min(jnp.where(later, ids[None, :], N_EXPERTS), axis=1)
    nxt = jnp.where(nxt == N_EXPERTS, -1, nxt)
    slot = (jnp.cumsum(has_rows.astype(jnp.int32)) - 1) % 2
    table = (blk, e, nval, first, pick(nxt), pick(slot))
    return tuple(col.astype(jnp.int32) for col in table)


def _combine_kernel(dest_ref, gate_ref, x1_ref, p_ref, y_ref, gple_ref, wpg_ref, wple_ref,
                    o_ref, buf, sems):
    tm = x1_ref.shape[0]
    i = pl.program_id(0)
    n = pl.num_programs(0)

    def issue(tile, slot):
        base = tile * (tm * TOP_K)

        def body(t, carry):
            for k in range(TOP_K):
                d = dest_ref[base + t * TOP_K + k]
                pltpu.make_async_copy(_token_tile(y_ref, d), _token_tile(buf.at[slot, k], t),
                                      sems.at[slot]).start(priority=k % 2)
            return carry

        lax.fori_loop(0, tm, body, 0, unroll=8)

    @pl.when(i == 0)
    def _():
        issue(0, 0)

    @pl.when(i + 1 < n)
    def _():
        issue(i + 1, (i + 1) % 2)

    slot = i % 2
    for k in range(TOP_K):
        _row_copy_wait(y_ref.at[pl.ds(0, tm * ROW_CHUNKS)], buf.at[slot, k], sems.at[slot])

    gates = gate_ref[...]
    moe = gates[:, 0:1] * _from_token_tiles(buf.at[slot, 0])
    for k in range(1, TOP_K):
        moe = moe + gates[:, k:k + 1] * _from_token_tiles(buf.at[slot, k])
    x2 = x1_ref[...] + moe
    hp = (x2 * _rms_scale(x2, D_MODEL) * gple_ref[...]).astype(MXU_DTYPE)
    ple_gate = jax.nn.sigmoid(_dot(hp, wpg_ref[...]))
    o_ref[...] = x2 + ple_gate * _dot(p_ref[...].astype(MXU_DTYPE), wple_ref[...])


def _combine(dest_flat, gates, x1, p2d, y, g_ple, w_ple_gate, w_ple):
    t = x1.shape[0]
    tm = TOKEN_TILE
    weights = (g_ple[None, :], w_ple_gate.astype(MXU_DTYPE), w_ple.astype(MXU_DTYPE))
    return pl.pallas_call(
        _combine_kernel,
        grid_spec=pltpu.PrefetchScalarGridSpec(
            num_scalar_prefetch=1,
            grid=(t // tm,),
            in_specs=[
                pl.BlockSpec((tm, TOP_K), lambda i, dest: (i, 0)),
                pl.BlockSpec((tm, D_MODEL), lambda i, dest: (i, 0)),
                pl.BlockSpec((tm, PLE_DIM), lambda i, dest: (i, 0)),
                pl.BlockSpec(memory_space=pl.ANY),
            ] + [pl.BlockSpec(w.shape, lambda i, dest, _nd=w.ndim: (0,) * _nd) for w in weights],
            out_specs=pl.BlockSpec((tm, D_MODEL), lambda i, dest: (i, 0)),
            scratch_shapes=[pltpu.VMEM((2, TOP_K, tm * ROW_CHUNKS, LANES), jnp.float32),
                            pltpu.SemaphoreType.DMA((2,))],
        ),
        out_shape=jax.ShapeDtypeStruct((t, D_MODEL), jnp.float32),
        compiler_params=pltpu.CompilerParams(
            dimension_semantics=("arbitrary",), vmem_limit_bytes=VMEM_LIMIT_BYTES),
        name="combine",
    )(dest_flat, gates, x1, p2d, y, *weights)


def kernel(x, p, positions, g_mix, w_in, g_q_lat, w_q_b, g_kv_lat, w_kv_b, g_q_head, g_k_nope, g_k_rope, w_pool, pool_scale, w_attn_out, w_pool_out, w_o, g_ffn, w_router, b_router, w_gate_up, b_gate_up, w_down, b_down, g_ple, w_ple_gate, w_ple):
    batch, seq, _ = x.shape
    tokens = batch * seq
    depth = g_mix.shape[0]
    assert seq % TOKEN_TILE == 0 and seq % ATTN_TILE == 0 and tokens % EXPERT_ROWS == 0
    x2d = x.reshape(tokens, D_MODEL)
    pos2d = positions.astype(jnp.float32).reshape(tokens, 1)
    for i in range(depth):
        weights = _premix_weights(g_mix[i], w_in[i], g_q_lat[i], w_q_b[i], g_kv_lat[i], w_kv_b[i],
                                  g_q_head[i], g_k_nope[i], g_k_rope[i], w_pool[i], pool_scale[i],
                                  w_pool_out[i])
        q, kt, v, ga, gp = _premix(x2d, pos2d, weights, batch, seq)
        attn = _attention(q, kt, v).reshape(tokens, N_HEADS * V_HEAD)
        x1, h2, idx, rank, gates, counts = _postmix(x2d, attn, ga, gp, w_attn_out[i], w_o[i], g_ffn[i],
                                                    w_router[i], b_router[i])
        dest = (idx * tokens + rank).reshape(-1)
        table = _expert_blocks(counts[0, :N_EXPERTS], tokens)
        xs = _dispatch(dest, h2, N_EXPERTS * tokens)
        y = _experts(table, xs, w_gate_up[i], b_gate_up[i], w_down[i], b_down[i])
        x2d = _combine(dest, gates, x1, p[i].reshape(tokens, PLE_DIM), y, g_ple[i], w_ple_gate[i], w_ple[i])
    return x2d.reshape(batch, seq, D_MODEL)
```

```python
import functools
import math

import jax
import jax.numpy as jnp
from jax import lax
from jax.experimental import pallas as pl
from jax.experimental.pallas import tpu as pltpu

D_MODEL = 1024
N_HEADS = 8
QK_NOPE = 64
QK_ROPE = 32
QK_HEAD = QK_NOPE + QK_ROPE
V_HEAD = 64
Q_LORA = 256
KV_LORA = 128
ROPE_THETA = 10000.0
EPS = 1e-6
POOL_WINDOWS = (2, 4, 8, 16)
POOL_GROUPS = 4
POOL_WIDTH = 512
POOL_GROUP_DIM = POOL_WIDTH // POOL_GROUPS
POOL_HALO = 16
N_EXPERTS = 32
TOP_K = 4
D_EXPERT = 1024
SWIGLU_ALPHA = 1.702
SWIGLU_LIMIT = 7.0
PLE_DIM = 256

LANES = 128
HEAD_SLAB = LANES
ROW_CHUNKS = D_MODEL // LANES
VMEM_LIMIT_BYTES = 56 * 1024 * 1024

MXU_DTYPE = jnp.bfloat16
NEG_BIG = -1e30
Q_SCALE = math.log2(math.e) / math.sqrt(QK_HEAD)

TOKEN_TILE = 512
ATTN_TILE = 512
EXPERT_ROWS = 512


def _dot(a, b):
    return jnp.dot(a, b, preferred_element_type=jnp.float32)


def _rms_scale(v, width):
    return lax.rsqrt(jnp.sum(v * v, axis=-1, keepdims=True) * (1.0 / width) + EPS)


_C_QLAT = 0
_C_KVLAT = _C_QLAT + Q_LORA
_C_KR = _C_KVLAT + KV_LORA
_C_KRR = _C_KR + HEAD_SLAB
_C_POOL = _C_KRR + HEAD_SLAB
_C_GA = _C_POOL + POOL_WIDTH
_C_GP = _C_GA + D_MODEL
_C_END = _C_GP + D_MODEL


def _premix_kernel(tiles_per_seq, x_ref, pos_ref, gmix_ref, w1_ref, gql_ref, wqs_ref, wqr_ref,
                   gkvl_ref, wks_ref, wv_ref, gqs_ref, gqr_ref, gks_ref, gkrs_ref, gkrr_ref,
                   invf_ref, wpool_ref, pscale_ref, wpo_ref,
                   q_ref, kt_ref, v_ref, ga_ref, gp_ref, halo_ref, proj_ref):
    @pl.when(pl.program_id(0) == 0)
    def _():
        halo_ref[...] = jnp.zeros_like(halo_ref)

    xv = x_ref[...]
    h = (xv * _rms_scale(xv, D_MODEL) * gmix_ref[...]).astype(MXU_DTYPE)
    proj_ref[...] = _dot(h, w1_ref[...])
    for section in _premix_sections(tiles_per_seq, proj_ref, pos_ref, gql_ref, wqs_ref, wqr_ref, gkvl_ref,
                                    wks_ref, wv_ref, gqs_ref, gqr_ref, gks_ref, gkrs_ref, gkrr_ref, invf_ref,
                                    wpool_ref, pscale_ref, wpo_ref, q_ref, kt_ref, v_ref, ga_ref, gp_ref,
                                    halo_ref):
        section()


def _premix_sections(tiles_per_seq, proj, pos_ref, gql_ref, wqs_ref, wqr_ref, gkvl_ref, wks_ref, wv_ref,
                     gqs_ref, gqr_ref, gks_ref, gkrs_ref, gkrr_ref, invf_ref, wpool_ref, pscale_ref,
                     wpo_ref, q_ref, kt_ref, v_ref, ga_ref, gp_ref, halo_ref):
    tm = proj.shape[0]
    si = pl.program_id(0) % tiles_per_seq
    shared = {}

    def rope_tables():
        if "cos" not in shared:
            ang = pos_ref[...] * invf_ref[...]
            shared["cos"] = jnp.cos(ang)
            shared["sin"] = jnp.sin(ang)
        return shared["cos"], shared["sin"]

    def q_project():
        q_lat = proj[:, _C_QLAT:_C_QLAT + Q_LORA]
        qln = (q_lat * _rms_scale(q_lat, Q_LORA) * gql_ref[...]).astype(MXU_DTYPE)
        shared["qs"] = _dot(qln, wqs_ref[...])
        shared["qr"] = _dot(qln, wqr_ref[...])

    def q_heads(heads):
        def run():
            cos_t, sin_t = rope_tables()
            lane = lax.broadcasted_iota(jnp.int32, (tm, LANES), 1)
            q_cos = gqs_ref[...] * jnp.where(lane < QK_NOPE, 1.0, cos_t)
            q_sin = gqr_ref[...] * sin_t
            for hd in heads:
                s = shared["qs"][:, hd * HEAD_SLAB:(hd + 1) * HEAD_SLAB]
                r = shared["qr"][:, hd * HEAD_SLAB:(hd + 1) * HEAD_SLAB]
                scale = _rms_scale(s, QK_HEAD) * Q_SCALE
                q_ref[0, hd] = ((s * q_cos + r * q_sin) * scale).astype(q_ref.dtype)
        return run

    def kv_project():
        cos_t, sin_t = rope_tables()
        kv_lat = proj[:, _C_KVLAT:_C_KVLAT + KV_LORA]
        kvn = (kv_lat * _rms_scale(kv_lat, KV_LORA) * gkvl_ref[...]).astype(MXU_DTYPE)
        shared["ks"] = _dot(kvn, wks_ref[...])
        v_ref[0] = _dot(kvn, wv_ref[...]).astype(v_ref.dtype)
        krs = proj[:, _C_KR:_C_KR + HEAD_SLAB]
        krr = proj[:, _C_KRR:_C_KRR + HEAD_SLAB]
        shared["k_rot"] = ((krs * (gkrs_ref[...] * cos_t) + krr * (gkrr_ref[...] * sin_t))
                           * _rms_scale(krs, QK_ROPE))

    def k_heads(heads):
        def run():
            for hd in heads:
                s = shared["ks"][:, hd * HEAD_SLAB:(hd + 1) * HEAD_SLAB]
                kh = s * _rms_scale(s, QK_NOPE) * gks_ref[...] + shared["k_rot"]
                kt_ref[0, hd] = kh.T.astype(kt_ref.dtype)
        return run

    def pool_groups(groups):
        def run():
            t_seq = si * tm + lax.broadcasted_iota(jnp.int32, (tm, 1), 0)
            for g in groups:
                w = POOL_WINDOWS[g]
                cols = slice(g * POOL_GROUP_DIM, (g + 1) * POOL_GROUP_DIM)
                u = proj[:, _C_POOL + cols.start:_C_POOL + cols.stop]
                history = jnp.where(si == 0, 0.0, halo_ref[:, cols])
                acc = jnp.concatenate([history, u], axis=0)
                halo_ref[:, cols] = u[tm - POOL_HALO:, :]
                shift = 1
                while shift < w:
                    acc = acc + pltpu.roll(acc, shift, axis=0)
                    shift *= 2
                count = jnp.minimum(t_seq + 1, w).astype(jnp.float32)
                pooled = acc[POOL_HALO:, :] / count - u
                shared[("mixed", g)] = _dot(pooled.astype(MXU_DTYPE), wpool_ref[g])
        return run

    def pool_out():
        mixed = jnp.concatenate([shared[("mixed", g)] for g in range(POOL_GROUPS)], axis=1) * pscale_ref[...]
        shared["y_pool"] = _dot(mixed.astype(MXU_DTYPE), wpo_ref[...])

    def gate_attn():
        ga_ref[...] = jax.nn.sigmoid(proj[:, _C_GA:_C_GA + D_MODEL]).astype(ga_ref.dtype)

    def gate_pool():
        gp_ref[...] = (jax.nn.sigmoid(proj[:, _C_GP:_C_GP + D_MODEL]) * shared["y_pool"]).astype(gp_ref.dtype)

    half = N_HEADS // 2
    return [q_project, kv_project, q_heads(range(half)), q_heads(range(half, N_HEADS)),
            k_heads(range(half)), k_heads(range(half, N_HEADS)),
            pool_groups((0, 1)), pool_groups((2, 3)), pool_out, gate_attn, gate_pool]


def _premix_weights(g_mix, w_in, g_q_lat, w_q_b, g_kv_lat, w_kv_b, g_q_head, g_k_nope, g_k_rope,
                    w_pool, pool_scale, w_pool_out):
    f32 = jnp.float32
    half = QK_ROPE // 2
    w_q_lat = w_in[:, :Q_LORA]
    w_kv_lat = w_in[:, Q_LORA:Q_LORA + KV_LORA]
    w_kr = w_in[:, Q_LORA + KV_LORA:Q_LORA + KV_LORA + QK_ROPE]
    off = Q_LORA + KV_LORA + QK_ROPE
    w_pool_in = w_in[:, off:off + POOL_WIDTH]
    w_ga = w_in[:, off + POOL_WIDTH:off + POOL_WIDTH + D_MODEL]
    w_gp = w_in[:, off + POOL_WIDTH + D_MODEL:]

    kr_s = jnp.zeros((D_MODEL, HEAD_SLAB), f32).at[:, QK_NOPE:QK_NOPE + QK_ROPE].set(w_kr)
    kr_r = jnp.zeros((D_MODEL, HEAD_SLAB), f32)
    kr_r = kr_r.at[:, QK_NOPE:QK_NOPE + half].set(-w_kr[:, half:])
    kr_r = kr_r.at[:, QK_NOPE + half:QK_NOPE + QK_ROPE].set(w_kr[:, :half])
    w1 = jnp.concatenate([w_q_lat, w_kv_lat, kr_s, kr_r, w_pool_in, w_ga, w_gp], axis=1)

    wq = w_q_b.reshape(Q_LORA, N_HEADS, QK_HEAD)
    wq_s = jnp.zeros((Q_LORA, N_HEADS, HEAD_SLAB), f32).at[:, :, :QK_HEAD].set(wq)
    wq_r = jnp.zeros((Q_LORA, N_HEADS, HEAD_SLAB), f32)
    wq_r = wq_r.at[:, :, QK_NOPE:QK_NOPE + half].set(-wq[:, :, QK_NOPE + half:])
    wq_r = wq_r.at[:, :, QK_NOPE + half:QK_HEAD].set(wq[:, :, QK_NOPE:QK_NOPE + half])
    wkv = w_kv_b.reshape(KV_LORA, N_HEADS, QK_NOPE + V_HEAD)
    wk_s = jnp.zeros((KV_LORA, N_HEADS, HEAD_SLAB), f32).at[:, :, :QK_NOPE].set(wkv[:, :, :QK_NOPE])
    wv = wkv[:, :, QK_NOPE:].reshape(KV_LORA, N_HEADS * V_HEAD)

    def slab(vals, start):
        return jnp.zeros((1, HEAD_SLAB), f32).at[0, start:start + vals.shape[0]].set(vals)

    gq_s = slab(g_q_head, 0)
    gq_r = slab(jnp.concatenate([g_q_head[QK_NOPE + half:], g_q_head[QK_NOPE:QK_NOPE + half]]), QK_NOPE)
    gk_s = slab(g_k_nope, 0)
    gkr_s = slab(g_k_rope, QK_NOPE)
    gkr_r = slab(jnp.concatenate([g_k_rope[half:], g_k_rope[:half]]), QK_NOPE)
    inv_freq = ROPE_THETA ** (-jnp.arange(0, QK_ROPE, 2, dtype=f32) / QK_ROPE)
    invf = jnp.tile(inv_freq, LANES // half)[None, :]
    bf = MXU_DTYPE
    return (g_mix[None, :], w1.astype(bf), g_q_lat[None, :],
            wq_s.reshape(Q_LORA, -1).astype(bf), wq_r.reshape(Q_LORA, -1).astype(bf),
            g_kv_lat[None, :], wk_s.reshape(KV_LORA, -1).astype(bf), wv.astype(bf),
            gq_s, gq_r, gk_s, gkr_s, gkr_r, invf, w_pool.astype(bf), pool_scale[None, :],
            w_pool_out.astype(bf))


def _full_spec(arr):
    nd = arr.ndim
    return pl.BlockSpec(arr.shape, lambda i, _nd=nd: (0,) * _nd)


def _premix(x2d, pos2d, weights, batch, seq):
    t = x2d.shape[0]
    tm = TOKEN_TILE
    tiles_per_seq = seq // tm
    in_specs = [pl.BlockSpec((tm, D_MODEL), lambda i: (i, 0)),
                pl.BlockSpec((tm, 1), lambda i: (i, 0))] + [_full_spec(w) for w in weights]
    out_shape = (
        jax.ShapeDtypeStruct((batch, N_HEADS, seq, HEAD_SLAB), MXU_DTYPE),
        jax.ShapeDtypeStruct((batch, N_HEADS, HEAD_SLAB, seq), MXU_DTYPE),
        jax.ShapeDtypeStruct((batch, seq, N_HEADS * V_HEAD), MXU_DTYPE),
        jax.ShapeDtypeStruct((t, D_MODEL), MXU_DTYPE),
        jax.ShapeDtypeStruct((t, D_MODEL), MXU_DTYPE),
    )
    out_specs = (
        pl.BlockSpec((1, N_HEADS, tm, HEAD_SLAB), lambda i: (i // tiles_per_seq, 0, i % tiles_per_seq, 0)),
        pl.BlockSpec((1, N_HEADS, HEAD_SLAB, tm), lambda i: (i // tiles_per_seq, 0, 0, i % tiles_per_seq)),
        pl.BlockSpec((1, tm, N_HEADS * V_HEAD), lambda i: (i // tiles_per_seq, i % tiles_per_seq, 0)),
        pl.BlockSpec((tm, D_MODEL), lambda i: (i, 0)),
        pl.BlockSpec((tm, D_MODEL), lambda i: (i, 0)),
    )
    return pl.pallas_call(
        functools.partial(_premix_kernel, tiles_per_seq),
        grid=(t // tm,),
        in_specs=in_specs,
        out_specs=out_specs,
        out_shape=out_shape,
        scratch_shapes=[pltpu.VMEM((POOL_HALO, POOL_WIDTH), jnp.float32),
                        pltpu.VMEM((tm, _C_END), jnp.float32)],
        compiler_params=pltpu.CompilerParams(
            dimension_semantics=("arbitrary",), vmem_limit_bytes=VMEM_LIMIT_BYTES),
        name="premix",
    )(x2d, pos2d, *weights)


def _attention_kernel(q_ref, kt_ref, v_ref, o_ref, va_ref, vb_ref, s_ref, m_ref, acc_ref):
    tq = q_ref.shape[2]
    tk = tq
    qi = pl.program_id(2)

    ones_lane = (V_HEAD, 0)

    @pl.when(qi == 0)
    def _():
        v = v_ref[0].astype(jnp.float32)
        lane = lax.broadcasted_iota(jnp.int32, v.shape, 1)
        pad_a = (lane == ones_lane[0]).astype(jnp.float32)
        pad_b = (lane == ones_lane[1]).astype(jnp.float32)
        va_ref[...] = jnp.where(lane < V_HEAD, v, pad_a).astype(va_ref.dtype)
        vb_ref[...] = jnp.where(lane >= V_HEAD, v, pad_b).astype(vb_ref.dtype)

    vsel = (va_ref, vb_ref)
    heads = range(2)

    def scores(blk, slot):
        start = pl.multiple_of(blk * tk, tk)
        for hh in heads:
            s_ref[slot, hh] = _dot(q_ref[0, hh], kt_ref[0, hh, :, pl.ds(start, tk)])

    def softmax_pv(blk, slot, masked):
        start = pl.multiple_of(blk * tk, tk)
        for hh in heads:
            if masked:
                row = lax.broadcasted_iota(jnp.int32, (tq, tk), 0)
                col = lax.broadcasted_iota(jnp.int32, (tq, tk), 1)
                s_ref[slot, hh] = jnp.where(row >= col, s_ref[slot, hh], NEG_BIG)
            m_old = m_ref[hh]
            m_new = jnp.maximum(m_old, jnp.max(s_ref[slot, hh], axis=-1, keepdims=True))
            p = jnp.exp2(s_ref[slot, hh] - m_new).astype(MXU_DTYPE)
            acc_ref[hh] = jnp.exp2(m_old - m_new) * acc_ref[hh] + _dot(p, vsel[hh][pl.ds(start, tk), :])
            m_ref[hh] = m_new

    m_ref[...] = jnp.full(m_ref.shape, NEG_BIG, jnp.float32)
    acc_ref[...] = jnp.zeros(acc_ref.shape, jnp.float32)
    scores(0, 0)

    def two_blocks(jj, carry):
        scores(2 * jj + 1, 1)
        softmax_pv(2 * jj, 0, False)
        scores(2 * jj + 2, 0)
        softmax_pv(2 * jj + 1, 1, False)
        return carry

    lax.fori_loop(0, qi // 2, two_blocks, 0)

    @pl.when(qi % 2 == 0)
    def _():
        softmax_pv(qi, 0, True)

    @pl.when(qi % 2 == 1)
    def _():
        scores(qi, 1)
        softmax_pv(qi - 1, 0, False)
        softmax_pv(qi, 1, True)

    lane = lax.broadcasted_iota(jnp.int32, (tq, HEAD_SLAB), 1)
    acc_a = acc_ref[0]
    acc_b = acc_ref[1]
    out_a = acc_a / acc_a[:, ones_lane[0]:ones_lane[0] + 1]
    out_b = acc_b / acc_b[:, ones_lane[1]:ones_lane[1] + 1]
    o_ref[0] = jnp.where(lane < V_HEAD, out_a, out_b).astype(o_ref.dtype)


def _attention(q, kt, v):
    batch, _, seq, _ = q.shape
    tq = ATTN_TILE
    pairs = N_HEADS // 2
    return pl.pallas_call(
        _attention_kernel,
        grid=(batch, pairs, seq // tq),
        in_specs=[
            pl.BlockSpec((1, 2, tq, HEAD_SLAB), lambda b, p, i: (b, p, i, 0)),
            pl.BlockSpec((1, 2, HEAD_SLAB, seq), lambda b, p, i: (b, p, 0, 0)),
            pl.BlockSpec((1, seq, 2 * V_HEAD), lambda b, p, i: (b, 0, p)),
        ],
        out_specs=pl.BlockSpec((1, tq, 2 * V_HEAD), lambda b, p, i: (b, i, p)),
        out_shape=jax.ShapeDtypeStruct((batch, seq, N_HEADS * V_HEAD), MXU_DTYPE),
        scratch_shapes=[pltpu.VMEM((seq, 2 * V_HEAD), MXU_DTYPE), pltpu.VMEM((seq, 2 * V_HEAD), MXU_DTYPE),
                        pltpu.VMEM((2, 2, tq, tq), jnp.float32),
                        pltpu.VMEM((2, tq, 1), jnp.float32),
                        pltpu.VMEM((2, tq, HEAD_SLAB), jnp.float32)],
        compiler_params=pltpu.CompilerParams(
            dimension_semantics=("arbitrary", "arbitrary", "arbitrary"), vmem_limit_bytes=VMEM_LIMIT_BYTES),
        name="attention",
    )(q, kt, v)


def _row_copy_wait(src_ref, dst_ref, sem):
    pltpu.make_async_copy(src_ref, dst_ref, sem).wait()


def _to_token_tiles(dst_ref, rows2d):
    rows = rows2d.shape[0]
    for c in range(ROW_CHUNKS):
        dst_ref[pl.ds(c, rows, stride=ROW_CHUNKS), :] = rows2d[:, c * LANES:(c + 1) * LANES]


def _from_token_tiles(src_ref):
    rows = src_ref.shape[0] // ROW_CHUNKS
    return jnp.concatenate([src_ref[pl.ds(c, rows, stride=ROW_CHUNKS), :] for c in range(ROW_CHUNKS)], axis=1)


def _token_tile(ref, index):
    return ref.at[pl.ds(pl.multiple_of(index * ROW_CHUNKS, ROW_CHUNKS), ROW_CHUNKS)]


SEL_ROWS = 8


def _route_kernel(tokens, x_ref, at_ref, ga_ref, gp_ref, wao_ref, wo_ref, gffn_ref, wrh_ref, wrl_ref, br_ref,
                  x1_ref, dest_ref, gate_ref, cnt_ref, xs_ref,
                  carry_ref, upper_ref, tiles_ref, stage_ref, dest_smem, stage_sem, sems):
    tm = x_ref.shape[0]
    i = pl.program_id(0)
    n = pl.num_programs(0)

    @pl.when(i == 0)
    def _():
        carry_ref[...] = jnp.zeros_like(carry_ref)
        a_i = lax.broadcasted_iota(jnp.int32, (tm, tm), 0)
        b_i = lax.broadcasted_iota(jnp.int32, (tm, tm), 1)
        upper_ref[...] = (a_i < b_i).astype(upper_ref.dtype)

    y_attn = _dot(at_ref[...], wao_ref[...])
    merged = ga_ref[...].astype(jnp.float32) * y_attn + gp_ref[...].astype(jnp.float32)
    x1_ref[...] = x_ref[...] + _dot(merged.astype(MXU_DTYPE), wo_ref[...])
    x1 = x1_ref[...]
    h2 = x1 * _rms_scale(x1, D_MODEL) * gffn_ref[...]
    slot = i % 2
    _to_token_tiles(tiles_ref.at[slot], h2)

    h_hi = h2.astype(MXU_DTYPE)
    h_lo = (h2 - h_hi.astype(jnp.float32)).astype(MXU_DTYPE)
    logits = _dot(h_hi, wrh_ref[...]) + (_dot(h_lo, wrh_ref[...]) + _dot(h_hi, wrl_ref[...])) + br_ref[...]
    work = logits.T[:N_EXPERTS]

    expert = lax.broadcasted_iota(jnp.int32, (N_EXPERTS, tm), 0)
    vals, idxs = [], []
    for _ in range(TOP_K):
        mx = jnp.max(work, axis=0, keepdims=True)
        ix = jnp.min(jnp.where(work == mx, expert, N_EXPERTS), axis=0, keepdims=True)
        vals.append(mx)
        idxs.append(ix)
        work = jnp.where(expert == ix, -jnp.inf, work)
    exps = [jnp.exp(v - vals[0]) for v in vals]
    denom = exps[0] + exps[1] + exps[2] + exps[3]

    onehot = jnp.zeros((N_EXPERTS, tm), jnp.float32)
    for ix in idxs:
        onehot = onehot + (expert == ix).astype(jnp.float32)
    before = carry_ref[...] + _dot(onehot.astype(MXU_DTYPE), upper_ref[...])
    carry_ref[...] = carry_ref[...] + jnp.sum(onehot, axis=1, keepdims=True)
    cnt_ref[...] = carry_ref[...].astype(jnp.int32)

    sel = lax.broadcasted_iota(jnp.int32, (SEL_ROWS, tm), 0)
    dest = jnp.zeros((SEL_ROWS, tm), jnp.int32)
    gate = jnp.zeros((SEL_ROWS, tm), jnp.float32)
    for k, ix in enumerate(idxs):
        rank = jnp.sum(jnp.where(expert == ix, before, 0.0), axis=0, keepdims=True).astype(jnp.int32)
        dest = jnp.where(sel == k, ix * tokens + rank, dest)
        gate = jnp.where(sel == k, exps[k] / denom, gate)
    dest_ref[...] = dest
    gate_ref[...] = gate

    stage_ref[...] = dest
    to_smem = pltpu.make_async_copy(stage_ref, dest_smem, stage_sem)
    to_smem.start()
    to_smem.wait()

    def issue(t, carry):
        for k in range(TOP_K):
            pltpu.make_async_copy(_token_tile(tiles_ref.at[slot], t), _token_tile(xs_ref, dest_smem[k, t]),
                                  sems.at[slot]).start(priority=k % 2)
        return carry

    lax.fori_loop(0, tm, issue, 0, unroll=8)

    def wait_tile(s):
        for _ in range(TOP_K):
            _row_copy_wait(tiles_ref.at[s], xs_ref.at[pl.ds(0, tm * ROW_CHUNKS)], sems.at[s])

    @pl.when(i > 0)
    def _():
        wait_tile(1 - slot)

    @pl.when(i == n - 1)
    def _():
        wait_tile(slot)


def _route(x2d, attn2d, ga, gp, w_attn_out, w_o, g_ffn, w_router, b_router):
    t = x2d.shape[0]
    tm = TOKEN_TILE
    f32 = jnp.float32
    wr = jnp.zeros((D_MODEL, LANES), f32).at[:, :N_EXPERTS].set(w_router)
    wr_hi = wr.astype(MXU_DTYPE)
    wr_lo = (wr - wr_hi.astype(f32)).astype(MXU_DTYPE)
    br = jnp.full((1, LANES), NEG_BIG, f32).at[0, :N_EXPERTS].set(b_router)
    weights = (w_attn_out.astype(MXU_DTYPE), w_o.astype(MXU_DTYPE), g_ffn[None, :], wr_hi, wr_lo, br)
    row_spec = lambda width: pl.BlockSpec((tm, width), lambda i: (i, 0))
    sel_spec = pl.BlockSpec((SEL_ROWS, tm), lambda i: (0, i))
    return pl.pallas_call(
        functools.partial(_route_kernel, t),
        grid=(t // tm,),
        in_specs=[row_spec(D_MODEL), row_spec(N_HEADS * V_HEAD), row_spec(D_MODEL), row_spec(D_MODEL)]
        + [_full_spec(w) for w in weights],
        out_specs=(row_spec(D_MODEL), sel_spec, sel_spec,
                   pl.BlockSpec((N_EXPERTS, 1), lambda i: (0, 0)),
                   pl.BlockSpec(memory_space=pl.ANY)),
        out_shape=(
            jax.ShapeDtypeStruct((t, D_MODEL), f32),
            jax.ShapeDtypeStruct((SEL_ROWS, t), jnp.int32),
            jax.ShapeDtypeStruct((SEL_ROWS, t), f32),
            jax.ShapeDtypeStruct((N_EXPERTS, 1), jnp.int32),
            jax.ShapeDtypeStruct((N_EXPERTS * t * ROW_CHUNKS, LANES), f32),
        ),
        scratch_shapes=[pltpu.VMEM((N_EXPERTS, 1), f32),
                        pltpu.VMEM((tm, tm), MXU_DTYPE),
                        pltpu.VMEM((2, tm * ROW_CHUNKS, LANES), f32),
                        pltpu.VMEM((SEL_ROWS, tm), jnp.int32),
                        pltpu.SMEM((SEL_ROWS, tm), jnp.int32),
                        pltpu.SemaphoreType.DMA(()),
                        pltpu.SemaphoreType.DMA((2,))],
        compiler_params=pltpu.CompilerParams(
            dimension_semantics=("arbitrary",), vmem_limit_bytes=VMEM_LIMIT_BYTES, has_side_effects=True),
        name="route",
    )(x2d, attn2d, ga, gp, *weights)


def _experts_kernel(blk_ref, bexp_ref, nval_ref, first_ref, next_ref, slot_ref,
                    xs_ref, wgu_ref, bgu_ref, wd_ref, bd_ref,
                    y_ref, wgu_f32, wd_f32, wgu_bf, wd_bf, sems):
    i = pl.program_id(0)
    rows = xs_ref.shape[0] // ROW_CHUNKS

    def weight_copies(expert, slot):
        return (pltpu.make_async_copy(wgu_ref.at[expert], wgu_f32.at[slot], sems.at[slot, 0]),
                pltpu.make_async_copy(wd_ref.at[expert], wd_f32.at[slot], sems.at[slot, 1]))

    @pl.when(i == 0)
    def _():
        for copy in weight_copies(bexp_ref[0], 0):
            copy.start()

    @pl.when(first_ref[i] == 1)
    def _():
        slot = slot_ref[i]

        @pl.when(next_ref[i] >= 0)
        def _():
            for copy in weight_copies(next_ref[i], 1 - slot):
                copy.start()

        for copy in weight_copies(bexp_ref[i], slot):
            copy.wait()
        wgu_bf[...] = wgu_f32[slot].astype(MXU_DTYPE)
        wd_bf[...] = wd_f32[slot].astype(MXU_DTYPE)

    def ffn(n_rows):
        tiles = pl.ds(0, n_rows * ROW_CHUNKS)
        row = lax.broadcasted_iota(jnp.int32, (n_rows, 1), 0)
        x = jnp.where(row < nval_ref[i], _from_token_tiles(xs_ref.at[tiles]), 0.0).astype(MXU_DTYPE)
        gu = _dot(x, wgu_bf[...]) + bgu_ref[0]
        gate = jnp.minimum(gu[:, :D_EXPERT], SWIGLU_LIMIT)
        up = jnp.clip(gu[:, D_EXPERT:], -SWIGLU_LIMIT, SWIGLU_LIMIT)
        act = (up + 1.0) * gate * jax.nn.sigmoid(SWIGLU_ALPHA * gate)
        _to_token_tiles(y_ref.at[tiles], _dot(act.astype(MXU_DTYPE), wd_bf[...]) + bd_ref[0])

    @pl.when(nval_ref[i] > rows // 2)
    def _():
        ffn(rows)

    @pl.when((nval_ref[i] > 0) & (nval_ref[i] <= rows // 2))
    def _():
        ffn(rows // 2)


def _experts(table, xs, w_gate_up, b_gate_up, w_down, b_down):
    rows = EXPERT_ROWS
    nb = table[0].shape[0]
    row_block = pl.BlockSpec((rows * ROW_CHUNKS, LANES), lambda i, blk, *_: (blk[i], 0))
    return pl.pallas_call(
        _experts_kernel,
        grid_spec=pltpu.PrefetchScalarGridSpec(
            num_scalar_prefetch=len(table),
            grid=(nb,),
            in_specs=[
                row_block,
                pl.BlockSpec(memory_space=pl.ANY),
                pl.BlockSpec((1, 1, 2 * D_EXPERT), lambda i, blk, bexp, *_: (bexp[i], 0, 0)),
                pl.BlockSpec(memory_space=pl.ANY),
                pl.BlockSpec((1, 1, D_MODEL), lambda i, blk, bexp, *_: (bexp[i], 0, 0)),
            ],
            out_specs=row_block,
            scratch_shapes=[pltpu.VMEM((2, D_MODEL, 2 * D_EXPERT), jnp.float32),
                            pltpu.VMEM((2, D_EXPERT, D_MODEL), jnp.float32),
                            pltpu.VMEM((D_MODEL, 2 * D_EXPERT), MXU_DTYPE),
                            pltpu.VMEM((D_EXPERT, D_MODEL), MXU_DTYPE),
                            pltpu.SemaphoreType.DMA((2, 2))],
        ),
        out_shape=jax.ShapeDtypeStruct(xs.shape, jnp.float32),
        compiler_params=pltpu.CompilerParams(
            dimension_semantics=("arbitrary",), vmem_limit_bytes=VMEM_LIMIT_BYTES),
        name="experts",
    )(*table, xs, w_gate_up, b_gate_up[:, None, :], w_down, b_down[:, None, :])


def _expert_blocks(counts, tokens):
    rows = EXPERT_ROWS
    nb_max = (tokens * TOP_K) // rows + N_EXPERTS
    nblk = (counts + rows - 1) // rows
    ends = jnp.cumsum(nblk)
    starts = ends - nblk
    total = ends[-1]
    step = jnp.arange(nb_max, dtype=jnp.int32)
    i = jnp.minimum(step, total - 1)[:, None]
    owner = ((i >= starts[None, :]) & (i < ends[None, :])).astype(jnp.int32)
    pick = lambda v: jnp.sum(owner * v[None, :], axis=1)
    e = pick(jnp.arange(N_EXPERTS, dtype=jnp.int32))
    j = i[:, 0] - pick(starts)
    active = step < total
    blk = e * (tokens // rows) + j
    nval = jnp.where(active, jnp.clip(pick(counts) - j * rows, 0, rows), 0)
    first = jnp.where(active & (j == 0), 1, 0)
    ids = jnp.arange(N_EXPERTS, dtype=jnp.int32)
    has_rows = nblk > 0
    later = has_rows[None, :] & (ids[None, :] > ids[:, None])
    nxt = jnp.min(jnp.where(later, ids[None, :], N_EXPERTS), axis=1)
    nxt = jnp.where(nxt == N_EXPERTS, -1, nxt)
    slot = (jnp.cumsum(has_rows.astype(jnp.int32)) - 1) % 2
    table = (blk, e, nval, first, pick(nxt), pick(slot))
    return tuple(col.astype(jnp.int32) for col in table)


def _combine_kernel(tokens, dest_ref, gate_ref, x1_ref, p_ref, y_ref, gple_ref, wpg_ref, wple_ref,
                    o_ref, buf, sems):
    tm = x1_ref.shape[0]
    i = pl.program_id(0)
    n = pl.num_programs(0)

    def issue(tile, slot):
        base = tile * tm

        def body(t, carry):
            for k in range(TOP_K):
                d = dest_ref[k * tokens + base + t]
                pltpu.make_async_copy(_token_tile(y_ref, d), _token_tile(buf.at[slot, k], t),
                                      sems.at[slot]).start(priority=k % 2)
            return carry

        lax.fori_loop(0, tm, body, 0, unroll=8)

    @pl.when(i == 0)
    def _():
        issue(0, 0)

    @pl.when(i + 1 < n)
    def _():
        issue(i + 1, (i + 1) % 2)

    slot = i % 2
    for k in range(TOP_K):
        _row_copy_wait(y_ref.at[pl.ds(0, tm * ROW_CHUNKS)], buf.at[slot, k], sems.at[slot])

    gates = gate_ref[...]
    moe = gates[:, 0:1] * _from_token_tiles(buf.at[slot, 0])
    for k in range(1, TOP_K):
        moe = moe + gates[:, k:k + 1] * _from_token_tiles(buf.at[slot, k])
    x2 = x1_ref[...] + moe
    hp = (x2 * _rms_scale(x2, D_MODEL) * gple_ref[...]).astype(MXU_DTYPE)
    ple_gate = jax.nn.sigmoid(_dot(hp, wpg_ref[...]))
    o_ref[...] = x2 + ple_gate * _dot(p_ref[...].astype(MXU_DTYPE), wple_ref[...])


def _combine(dest_flat, gates, x1, p2d, y, g_ple, w_ple_gate, w_ple):
    t = x1.shape[0]
    tm = TOKEN_TILE
    weights = (g_ple[None, :], w_ple_gate.astype(MXU_DTYPE), w_ple.astype(MXU_DTYPE))
    return pl.pallas_call(
        functools.partial(_combine_kernel, t),
        grid_spec=pltpu.PrefetchScalarGridSpec(
            num_scalar_prefetch=1,
            grid=(t // tm,),
            in_specs=[
                pl.BlockSpec((tm, TOP_K), lambda i, dest: (i, 0)),
                pl.BlockSpec((tm, D_MODEL), lambda i, dest: (i, 0)),
                pl.BlockSpec((tm, PLE_DIM), lambda i, dest: (i, 0)),
                pl.BlockSpec(memory_space=pl.ANY),
            ] + [pl.BlockSpec(w.shape, lambda i, dest, _nd=w.ndim: (0,) * _nd) for w in weights],
            out_specs=pl.BlockSpec((tm, D_MODEL), lambda i, dest: (i, 0)),
            scratch_shapes=[pltpu.VMEM((2, TOP_K, tm * ROW_CHUNKS, LANES), jnp.float32),
                            pltpu.SemaphoreType.DMA((2,))],
        ),
        out_shape=jax.ShapeDtypeStruct((t, D_MODEL), jnp.float32),
        compiler_params=pltpu.CompilerParams(
            dimension_semantics=("arbitrary",), vmem_limit_bytes=VMEM_LIMIT_BYTES),
        name="combine",
    )(dest_flat, gates, x1, p2d, y, *weights)


def kernel(x, p, positions, g_mix, w_in, g_q_lat, w_q_b, g_kv_lat, w_kv_b, g_q_head, g_k_nope, g_k_rope, w_pool, pool_scale, w_attn_out, w_pool_out, w_o, g_ffn, w_router, b_router, w_gate_up, b_gate_up, w_down, b_down, g_ple, w_ple_gate, w_ple):
    batch, seq, _ = x.shape
    tokens = batch * seq
    depth = g_mix.shape[0]
    assert seq % TOKEN_TILE == 0 and seq % ATTN_TILE == 0 and tokens % EXPERT_ROWS == 0
    x2d = x.reshape(tokens, D_MODEL)
    pos2d = positions.astype(jnp.float32).reshape(tokens, 1)
    for i in range(depth):
        weights = _premix_weights(g_mix[i], w_in[i], g_q_lat[i], w_q_b[i], g_kv_lat[i], w_kv_b[i],
                                  g_q_head[i], g_k_nope[i], g_k_rope[i], w_pool[i], pool_scale[i],
                                  w_pool_out[i])
        q, kt, v, ga, gp = _premix(x2d, pos2d, weights, batch, seq)
        attn = _attention(q, kt, v).reshape(tokens, N_HEADS * V_HEAD)
        x1, dest, gates, counts, xs = _route(x2d, attn, ga, gp, w_attn_out[i], w_o[i], g_ffn[i],
                                             w_router[i], b_router[i])
        table = _expert_blocks(counts[:, 0], tokens)
        y = _experts(table, xs, w_gate_up[i], b_gate_up[i], w_down[i], b_down[i])
        x2d = _combine(dest[:TOP_K].reshape(-1), gates[:TOP_K].T, x1, p[i].reshape(tokens, PLE_DIM), y,
                       g_ple[i], w_ple_gate[i], w_ple[i])
    return x2d.reshape(batch, seq, D_MODEL)
```

```python
import functools
import math

import jax
import jax.numpy as jnp
from jax import lax
from jax.experimental import pallas as pl
from jax.experimental.pallas import tpu as pltpu

D_MODEL = 1024
N_HEADS = 8
QK_NOPE = 64
QK_ROPE = 32
QK_HEAD = QK_NOPE + QK_ROPE
V_HEAD = 64
Q_LORA = 256
KV_LORA = 128
ROPE_THETA = 10000.0
EPS = 1e-6
POOL_WINDOWS = (2, 4, 8, 16)
POOL_GROUPS = 4
POOL_WIDTH = 512
POOL_GROUP_DIM = POOL_WIDTH // POOL_GROUPS
POOL_HALO = 16
N_EXPERTS = 32
TOP_K = 4
D_EXPERT = 1024
SWIGLU_ALPHA = 1.702
SWIGLU_LIMIT = 7.0
PLE_DIM = 256

LANES = 128
HEAD_SLAB = LANES
ROW_CHUNKS = D_MODEL // LANES
VMEM_LIMIT_BYTES = 56 * 1024 * 1024

MXU_DTYPE = jnp.bfloat16
NEG_BIG = -1e30
Q_SCALE = math.log2(math.e) / math.sqrt(QK_HEAD)

TOKEN_TILE = 512
ATTN_TILE = 512
ATTN_HEADS = 4
EXPERT_ROWS = 512


def _dot(a, b):
    return jnp.dot(a, b, preferred_element_type=jnp.float32)


def _rms_scale(v, width):
    return lax.rsqrt(jnp.sum(v * v, axis=-1, keepdims=True) * (1.0 / width) + EPS)


_C_QLAT = 0
_C_KVLAT = _C_QLAT + Q_LORA
_C_KR = _C_KVLAT + KV_LORA
_C_KRR = _C_KR + HEAD_SLAB
_C_POOL = _C_KRR + HEAD_SLAB
_C_GA = _C_POOL + POOL_WIDTH
_C_GP = _C_GA + D_MODEL
_C_END = _C_GP + D_MODEL


def _premix_kernel(tiles_per_seq, x_ref, pos_ref, gmix_ref, w1_ref, gql_ref, wqs_ref, wqr_ref,
                   gkvl_ref, wks_ref, wv_ref, gqs_ref, gqr_ref, gks_ref, gkrs_ref, gkrr_ref,
                   invf_ref, wpool_ref, pscale_ref, wpo_ref,
                   q_ref, kt_ref, v_ref, ga_ref, gp_ref, halo_ref, proj_ref):
    @pl.when(pl.program_id(0) == 0)
    def _():
        halo_ref[...] = jnp.zeros_like(halo_ref)

    xv = x_ref[...]
    h = (xv * _rms_scale(xv, D_MODEL) * gmix_ref[...]).astype(MXU_DTYPE)
    proj_ref[...] = _dot(h, w1_ref[...])
    for section in _premix_sections(tiles_per_seq, proj_ref, pos_ref, gql_ref, wqs_ref, wqr_ref, gkvl_ref,
                                    wks_ref, wv_ref, gqs_ref, gqr_ref, gks_ref, gkrs_ref, gkrr_ref, invf_ref,
                                    wpool_ref, pscale_ref, wpo_ref, q_ref, kt_ref, v_ref, ga_ref, gp_ref,
                                    halo_ref):
        section()


def _premix_sections(tiles_per_seq, proj, pos_ref, gql_ref, wqs_ref, wqr_ref, gkvl_ref, wks_ref, wv_ref,
                     gqs_ref, gqr_ref, gks_ref, gkrs_ref, gkrr_ref, invf_ref, wpool_ref, pscale_ref,
                     wpo_ref, q_ref, kt_ref, v_ref, ga_ref, gp_ref, halo_ref):
    tm = proj.shape[0]
    si = pl.program_id(0) % tiles_per_seq
    shared = {}

    def rope_tables():
        if "cos" not in shared:
            ang = invf_ref[...] * pos_ref[...]
            expand = lambda tab: jnp.tile(tab, (LANES // tab.shape[0], 1)).T
            shared["cos"] = expand(jnp.cos(ang))
            shared["sin"] = expand(jnp.sin(ang))
        return shared["cos"], shared["sin"]

    def q_project():
        q_lat = proj[:, _C_QLAT:_C_QLAT + Q_LORA]
        qln = (q_lat * _rms_scale(q_lat, Q_LORA) * gql_ref[...]).astype(MXU_DTYPE)
        shared["qs"] = _dot(qln, wqs_ref[...])
        shared["qr"] = _dot(qln, wqr_ref[...])

    def q_heads(heads):
        def run():
            cos_t, sin_t = rope_tables()
            lane = lax.broadcasted_iota(jnp.int32, (tm, LANES), 1)
            q_cos = gqs_ref[...] * jnp.where(lane < QK_NOPE, 1.0, cos_t)
            q_sin = gqr_ref[...] * sin_t
            for hd in heads:
                s = shared["qs"][:, hd * HEAD_SLAB:(hd + 1) * HEAD_SLAB]
                r = shared["qr"][:, hd * HEAD_SLAB:(hd + 1) * HEAD_SLAB]
                scale = _rms_scale(s, QK_HEAD) * Q_SCALE
                q_ref[0, hd] = ((s * q_cos + r * q_sin) * scale).astype(q_ref.dtype)
        return run

    def kv_project():
        cos_t, sin_t = rope_tables()
        kv_lat = proj[:, _C_KVLAT:_C_KVLAT + KV_LORA]
        kvn = (kv_lat * _rms_scale(kv_lat, KV_LORA) * gkvl_ref[...]).astype(MXU_DTYPE)
        shared["ks"] = _dot(kvn, wks_ref[...])
        v_ref[0] = _dot(kvn, wv_ref[...]).astype(v_ref.dtype)
        krs = proj[:, _C_KR:_C_KR + HEAD_SLAB]
        krr = proj[:, _C_KRR:_C_KRR + HEAD_SLAB]
        shared["k_rot"] = ((krs * (gkrs_ref[...] * cos_t) + krr * (gkrr_ref[...] * sin_t))
                           * _rms_scale(krs, QK_ROPE))

    def k_heads(heads):
        def run():
            for hd in heads:
                s = shared["ks"][:, hd * HEAD_SLAB:(hd + 1) * HEAD_SLAB]
                kh = s * _rms_scale(s, QK_NOPE) * gks_ref[...] + shared["k_rot"]
                kt_ref[0, hd] = kh.T.astype(kt_ref.dtype)
        return run

    def pool_groups(groups):
        def run():
            t_seq = si * tm + lax.broadcasted_iota(jnp.int32, (tm, 1), 0)
            for g in groups:
                w = POOL_WINDOWS[g]
                cols = slice(g * POOL_GROUP_DIM, (g + 1) * POOL_GROUP_DIM)
                u = proj[:, _C_POOL + cols.start:_C_POOL + cols.stop]
                history = jnp.where(si == 0, 0.0, halo_ref[:, cols])
                acc = jnp.concatenate([history, u], axis=0)
                halo_ref[:, cols] = u[tm - POOL_HALO:, :]
                shift = 1
                while shift < w:
                    acc = acc + pltpu.roll(acc, shift, axis=0)
                    shift *= 2
                count = jnp.minimum(t_seq + 1, w).astype(jnp.float32)
                pooled = acc[POOL_HALO:, :] / count - u
                shared[("mixed", g)] = _dot(pooled.astype(MXU_DTYPE), wpool_ref[g])
        return run

    def pool_out():
        mixed = jnp.concatenate([shared[("mixed", g)] for g in range(POOL_GROUPS)], axis=1) * pscale_ref[...]
        shared["y_pool"] = _dot(mixed.astype(MXU_DTYPE), wpo_ref[...])

    def gate_attn():
        ga_ref[...] = jax.nn.sigmoid(proj[:, _C_GA:_C_GA + D_MODEL]).astype(ga_ref.dtype)

    def gate_pool():
        gp_ref[...] = (jax.nn.sigmoid(proj[:, _C_GP:_C_GP + D_MODEL]) * shared["y_pool"]).astype(gp_ref.dtype)

    half = N_HEADS // 2
    return [q_project, kv_project, q_heads(range(half)), q_heads(range(half, N_HEADS)),
            k_heads(range(half)), k_heads(range(half, N_HEADS)),
            pool_groups((0, 1)), pool_groups((2, 3)), pool_out, gate_attn, gate_pool]


def _premix_weights(g_mix, w_in, g_q_lat, w_q_b, g_kv_lat, w_kv_b, g_q_head, g_k_nope, g_k_rope,
                    w_pool, pool_scale, w_pool_out):
    f32 = jnp.float32
    half = QK_ROPE // 2
    w_q_lat = w_in[:, :Q_LORA]
    w_kv_lat = w_in[:, Q_LORA:Q_LORA + KV_LORA]
    w_kr = w_in[:, Q_LORA + KV_LORA:Q_LORA + KV_LORA + QK_ROPE]
    off = Q_LORA + KV_LORA + QK_ROPE
    w_pool_in = w_in[:, off:off + POOL_WIDTH]
    w_ga = w_in[:, off + POOL_WIDTH:off + POOL_WIDTH + D_MODEL]
    w_gp = w_in[:, off + POOL_WIDTH + D_MODEL:]

    kr_s = jnp.zeros((D_MODEL, HEAD_SLAB), f32).at[:, QK_NOPE:QK_NOPE + QK_ROPE].set(w_kr)
    kr_r = jnp.zeros((D_MODEL, HEAD_SLAB), f32)
    kr_r = kr_r.at[:, QK_NOPE:QK_NOPE + half].set(-w_kr[:, half:])
    kr_r = kr_r.at[:, QK_NOPE + half:QK_NOPE + QK_ROPE].set(w_kr[:, :half])
    w1 = jnp.concatenate([w_q_lat, w_kv_lat, kr_s, kr_r, w_pool_in, w_ga, w_gp], axis=1)

    wq = w_q_b.reshape(Q_LORA, N_HEADS, QK_HEAD)
    wq_s = jnp.zeros((Q_LORA, N_HEADS, HEAD_SLAB), f32).at[:, :, :QK_HEAD].set(wq)
    wq_r = jnp.zeros((Q_LORA, N_HEADS, HEAD_SLAB), f32)
    wq_r = wq_r.at[:, :, QK_NOPE:QK_NOPE + half].set(-wq[:, :, QK_NOPE + half:])
    wq_r = wq_r.at[:, :, QK_NOPE + half:QK_HEAD].set(wq[:, :, QK_NOPE:QK_NOPE + half])
    wkv = w_kv_b.reshape(KV_LORA, N_HEADS, QK_NOPE + V_HEAD)
    wk_s = jnp.zeros((KV_LORA, N_HEADS, HEAD_SLAB), f32).at[:, :, :QK_NOPE].set(wkv[:, :, :QK_NOPE])
    wv = wkv[:, :, QK_NOPE:].reshape(KV_LORA, N_HEADS * V_HEAD)

    def slab(vals, start):
        return jnp.zeros((1, HEAD_SLAB), f32).at[0, start:start + vals.shape[0]].set(vals)

    gq_s = slab(g_q_head, 0)
    gq_r = slab(jnp.concatenate([g_q_head[QK_NOPE + half:], g_q_head[QK_NOPE:QK_NOPE + half]]), QK_NOPE)
    gk_s = slab(g_k_nope, 0)
    gkr_s = slab(g_k_rope, QK_NOPE)
    gkr_r = slab(jnp.concatenate([g_k_rope[half:], g_k_rope[:half]]), QK_NOPE)
    inv_freq = ROPE_THETA ** (-jnp.arange(0, QK_ROPE, 2, dtype=f32) / QK_ROPE)
    invf = inv_freq[:, None]
    bf = MXU_DTYPE
    return (g_mix[None, :], w1.astype(bf), g_q_lat[None, :],
            wq_s.reshape(Q_LORA, -1).astype(bf), wq_r.reshape(Q_LORA, -1).astype(bf),
            g_kv_lat[None, :], wk_s.reshape(KV_LORA, -1).astype(bf), wv.astype(bf),
            gq_s, gq_r, gk_s, gkr_s, gkr_r, invf, w_pool.astype(bf), pool_scale[None, :],
            w_pool_out.astype(bf))


def _full_spec(arr):
    nd = arr.ndim
    return pl.BlockSpec(arr.shape, lambda i, _nd=nd: (0,) * _nd)


def _premix(x2d, pos2d, weights, batch, seq):
    t = x2d.shape[0]
    tm = TOKEN_TILE
    tiles_per_seq = seq // tm
    in_specs = [pl.BlockSpec((tm, D_MODEL), lambda i: (i, 0)),
                pl.BlockSpec((1, tm), lambda i: (0, i))] + [_full_spec(w) for w in weights]
    out_shape = (
        jax.ShapeDtypeStruct((batch, N_HEADS, seq, HEAD_SLAB), MXU_DTYPE),
        jax.ShapeDtypeStruct((batch, N_HEADS, HEAD_SLAB, seq), MXU_DTYPE),
        jax.ShapeDtypeStruct((batch, seq, N_HEADS * V_HEAD), MXU_DTYPE),
        jax.ShapeDtypeStruct((t, D_MODEL), MXU_DTYPE),
        jax.ShapeDtypeStruct((t, D_MODEL), MXU_DTYPE),
    )
    out_specs = (
        pl.BlockSpec((1, N_HEADS, tm, HEAD_SLAB), lambda i: (i // tiles_per_seq, 0, i % tiles_per_seq, 0)),
        pl.BlockSpec((1, N_HEADS, HEAD_SLAB, tm), lambda i: (i // tiles_per_seq, 0, 0, i % tiles_per_seq)),
        pl.BlockSpec((1, tm, N_HEADS * V_HEAD), lambda i: (i // tiles_per_seq, i % tiles_per_seq, 0)),
        pl.BlockSpec((tm, D_MODEL), lambda i: (i, 0)),
        pl.BlockSpec((tm, D_MODEL), lambda i: (i, 0)),
    )
    return pl.pallas_call(
        functools.partial(_premix_kernel, tiles_per_seq),
        grid=(t // tm,),
        in_specs=in_specs,
        out_specs=out_specs,
        out_shape=out_shape,
        scratch_shapes=[pltpu.VMEM((POOL_HALO, POOL_WIDTH), jnp.float32),
                        pltpu.VMEM((tm, _C_END), jnp.float32)],
        compiler_params=pltpu.CompilerParams(
            dimension_semantics=("arbitrary",), vmem_limit_bytes=VMEM_LIMIT_BYTES),
        name="premix",
    )(x2d, pos2d, *weights)


def _attention_kernel(q_ref, kt_ref, v_ref, o_ref, vsel_ref, s_ref, m_ref, acc_ref):
    tq = q_ref.shape[2]
    tk = tq
    qi = pl.program_id(2)
    heads = range(q_ref.shape[1])

    ones_lane = (V_HEAD, 0)

    @pl.when(qi == 0)
    def _():
        for hh in heads:
            v = v_ref[0, :, (hh // 2) * HEAD_SLAB:(hh // 2 + 1) * HEAD_SLAB].astype(jnp.float32)
            lane = lax.broadcasted_iota(jnp.int32, v.shape, 1)
            own = (lane < V_HEAD) if hh % 2 == 0 else (lane >= V_HEAD)
            pad = (lane == ones_lane[hh % 2]).astype(jnp.float32)
            vsel_ref[hh] = jnp.where(own, v, pad).astype(vsel_ref.dtype)

    def scores(blk, slot):
        start = pl.multiple_of(blk * tk, tk)
        for hh in heads:
            s_ref[slot, hh] = _dot(q_ref[0, hh], kt_ref[0, hh, :, pl.ds(start, tk)])

    def softmax_pv(blk, slot, masked):
        start = pl.multiple_of(blk * tk, tk)
        for hh in heads:
            if masked:
                row = lax.broadcasted_iota(jnp.int32, (tq, tk), 0)
                col = lax.broadcasted_iota(jnp.int32, (tq, tk), 1)
                s_ref[slot, hh] = jnp.where(row >= col, s_ref[slot, hh], NEG_BIG)
            m_old = m_ref[hh]
            m_new = jnp.maximum(m_old, jnp.max(s_ref[slot, hh], axis=-1, keepdims=True))
            p = jnp.exp2(s_ref[slot, hh] - m_new).astype(MXU_DTYPE)
            acc_ref[hh] = jnp.exp2(m_old - m_new) * acc_ref[hh] + _dot(p, vsel_ref[hh, pl.ds(start, tk), :])
            m_ref[hh] = m_new

    m_ref[...] = jnp.full(m_ref.shape, NEG_BIG, jnp.float32)
    acc_ref[...] = jnp.zeros(acc_ref.shape, jnp.float32)
    scores(0, 0)

    def two_blocks(jj, carry):
        scores(2 * jj + 1, 1)
        softmax_pv(2 * jj, 0, False)
        scores(2 * jj + 2, 0)
        softmax_pv(2 * jj + 1, 1, False)
        return carry

    lax.fori_loop(0, qi // 2, two_blocks, 0)

    @pl.when(qi % 2 == 0)
    def _():
        softmax_pv(qi, 0, True)

    @pl.when(qi % 2 == 1)
    def _():
        scores(qi, 1)
        softmax_pv(qi - 1, 0, False)
        softmax_pv(qi, 1, True)

    lane = lax.broadcasted_iota(jnp.int32, (tq, HEAD_SLAB), 1)
    for pair in range(len(heads) // 2):
        acc_a = acc_ref[2 * pair]
        acc_b = acc_ref[2 * pair + 1]
        out_a = acc_a / acc_a[:, ones_lane[0]:ones_lane[0] + 1]
        out_b = acc_b / acc_b[:, ones_lane[1]:ones_lane[1] + 1]
        o_ref[0, :, pair * HEAD_SLAB:(pair + 1) * HEAD_SLAB] = jnp.where(lane < V_HEAD, out_a, out_b).astype(o_ref.dtype)


def _attention(q, kt, v):
    batch, _, seq, _ = q.shape
    tq = ATTN_TILE
    hg = ATTN_HEADS
    return pl.pallas_call(
        _attention_kernel,
        grid=(batch, N_HEADS // hg, seq // tq),
        in_specs=[
            pl.BlockSpec((1, hg, tq, HEAD_SLAB), lambda b, g, i: (b, g, i, 0)),
            pl.BlockSpec((1, hg, HEAD_SLAB, seq), lambda b, g, i: (b, g, 0, 0)),
            pl.BlockSpec((1, seq, hg * V_HEAD), lambda b, g, i: (b, 0, g)),
        ],
        out_specs=pl.BlockSpec((1, tq, hg * V_HEAD), lambda b, g, i: (b, i, g)),
        out_shape=jax.ShapeDtypeStruct((batch, seq, N_HEADS * V_HEAD), MXU_DTYPE),
        scratch_shapes=[pltpu.VMEM((hg, seq, HEAD_SLAB), MXU_DTYPE),
                        pltpu.VMEM((2, hg, tq, tq), jnp.float32),
                        pltpu.VMEM((hg, tq, 1), jnp.float32),
                        pltpu.VMEM((hg, tq, HEAD_SLAB), jnp.float32)],
        compiler_params=pltpu.CompilerParams(
            dimension_semantics=("arbitrary", "arbitrary", "arbitrary"), vmem_limit_bytes=VMEM_LIMIT_BYTES),
        name="attention",
    )(q, kt, v)


def _row_copy_wait(src_ref, dst_ref, sem):
    pltpu.make_async_copy(src_ref, dst_ref, sem).wait()


def _to_token_tiles(dst_ref, rows2d):
    rows = rows2d.shape[0]
    for c in range(ROW_CHUNKS):
        dst_ref[pl.ds(c, rows, stride=ROW_CHUNKS), :] = rows2d[:, c * LANES:(c + 1) * LANES]


def _from_token_tiles(src_ref):
    rows = src_ref.shape[0] // ROW_CHUNKS
    return jnp.concatenate([src_ref[pl.ds(c, rows, stride=ROW_CHUNKS), :] for c in range(ROW_CHUNKS)], axis=1)


def _token_tile(ref, index):
    return ref.at[pl.ds(pl.multiple_of(index * ROW_CHUNKS, ROW_CHUNKS), ROW_CHUNKS)]


SEL_ROWS = 8


def _route_kernel(tokens, x_ref, at_ref, ga_ref, gp_ref, wao_ref, wo_ref, gffn_ref, wrh_ref, wrl_ref, br_ref,
                  x1_ref, dest_ref, gate_ref, cnt_ref, xs_ref,
                  carry_ref, upper_ref, tiles_ref, stage_ref, dest_smem, stage_sem, sems):
    tm = x_ref.shape[0]
    i = pl.program_id(0)
    n = pl.num_programs(0)

    @pl.when(i == 0)
    def _():
        carry_ref[...] = jnp.zeros_like(carry_ref)
        a_i = lax.broadcasted_iota(jnp.int32, (tm, tm), 0)
        b_i = lax.broadcasted_iota(jnp.int32, (tm, tm), 1)
        upper_ref[...] = (a_i < b_i).astype(upper_ref.dtype)

    y_attn = _dot(at_ref[...], wao_ref[...])
    merged = ga_ref[...].astype(jnp.float32) * y_attn + gp_ref[...].astype(jnp.float32)
    x1_ref[...] = x_ref[...] + _dot(merged.astype(MXU_DTYPE), wo_ref[...])
    x1 = x1_ref[...]
    h2 = x1 * _rms_scale(x1, D_MODEL) * gffn_ref[...]
    slot = i % 2
    _to_token_tiles(tiles_ref.at[slot], h2)

    h_hi = h2.astype(MXU_DTYPE)
    h_lo = (h2 - h_hi.astype(jnp.float32)).astype(MXU_DTYPE)
    logits = _dot(h_hi, wrh_ref[...]) + (_dot(h_lo, wrh_ref[...]) + _dot(h_hi, wrl_ref[...])) + br_ref[...]
    work = logits.T[:N_EXPERTS]

    expert = lax.broadcasted_iota(jnp.int32, (N_EXPERTS, tm), 0)
    vals, idxs = [], []
    for _ in range(TOP_K):
        mx = jnp.max(work, axis=0, keepdims=True)
        ix = jnp.min(jnp.where(work == mx, expert, N_EXPERTS), axis=0, keepdims=True)
        vals.append(mx)
        idxs.append(ix)
        work = jnp.where(expert == ix, -jnp.inf, work)
    exps = [jnp.exp(v - vals[0]) for v in vals]
    denom = exps[0] + exps[1] + exps[2] + exps[3]

    onehot = jnp.zeros((N_EXPERTS, tm), jnp.float32)
    for ix in idxs:
        onehot = onehot + (expert == ix).astype(jnp.float32)
    before = carry_ref[...] + _dot(onehot.astype(MXU_DTYPE), upper_ref[...])
    carry_ref[...] = carry_ref[...] + jnp.sum(onehot, axis=1, keepdims=True)
    cnt_ref[...] = carry_ref[...].astype(jnp.int32)

    sel = lax.broadcasted_iota(jnp.int32, (SEL_ROWS, tm), 0)
    dest = jnp.zeros((SEL_ROWS, tm), jnp.int32)
    gate = jnp.zeros((SEL_ROWS, tm), jnp.float32)
    for k, ix in enumerate(idxs):
        rank = jnp.sum(jnp.where(expert == ix, before, 0.0), axis=0, keepdims=True).astype(jnp.int32)
        dest = jnp.where(sel == k, ix * tokens + rank, dest)
        gate = jnp.where(sel == k, exps[k] / denom, gate)
    dest_ref[...] = dest
    gate_ref[...] = gate

    stage_ref[...] = dest
    to_smem = pltpu.make_async_copy(stage_ref, dest_smem, stage_sem)
    to_smem.start()
    to_smem.wait()

    def issue(t, carry):
        for k in range(TOP_K):
            pltpu.make_async_copy(_token_tile(tiles_ref.at[slot], t), _token_tile(xs_ref, dest_smem[k, t]),
                                  sems.at[slot]).start(priority=k % 2)
        return carry

    lax.fori_loop(0, tm, issue, 0, unroll=8)

    def wait_tile(s):
        for _ in range(TOP_K):
            _row_copy_wait(tiles_ref.at[s], xs_ref.at[pl.ds(0, tm * ROW_CHUNKS)], sems.at[s])

    @pl.when(i > 0)
    def _():
        wait_tile(1 - slot)

    @pl.when(i == n - 1)
    def _():
        wait_tile(slot)


def _route(x2d, attn2d, ga, gp, w_attn_out, w_o, g_ffn, w_router, b_router):
    t = x2d.shape[0]
    tm = TOKEN_TILE
    f32 = jnp.float32
    wr = jnp.zeros((D_MODEL, LANES), f32).at[:, :N_EXPERTS].set(w_router)
    wr_hi = wr.astype(MXU_DTYPE)
    wr_lo = (wr - wr_hi.astype(f32)).astype(MXU_DTYPE)
    br = jnp.full((1, LANES), NEG_BIG, f32).at[0, :N_EXPERTS].set(b_router)
    weights = (w_attn_out.astype(MXU_DTYPE), w_o.astype(MXU_DTYPE), g_ffn[None, :], wr_hi, wr_lo, br)
    row_spec = lambda width: pl.BlockSpec((tm, width), lambda i: (i, 0))
    sel_spec = pl.BlockSpec((SEL_ROWS, tm), lambda i: (0, i))
    return pl.pallas_call(
        functools.partial(_route_kernel, t),
        grid=(t // tm,),
        in_specs=[row_spec(D_MODEL), row_spec(N_HEADS * V_HEAD), row_spec(D_MODEL), row_spec(D_MODEL)]
        + [_full_spec(w) for w in weights],
        out_specs=(row_spec(D_MODEL), sel_spec, sel_spec,
                   pl.BlockSpec((N_EXPERTS, 1), lambda i: (0, 0)),
                   pl.BlockSpec(memory_space=pl.ANY)),
        out_shape=(
            jax.ShapeDtypeStruct((t, D_MODEL), f32),
            jax.ShapeDtypeStruct((SEL_ROWS, t), jnp.int32),
            jax.ShapeDtypeStruct((SEL_ROWS, t), f32),
            jax.ShapeDtypeStruct((N_EXPERTS, 1), jnp.int32),
            jax.ShapeDtypeStruct((N_EXPERTS * t * ROW_CHUNKS, LANES), f32),
        ),
        scratch_shapes=[pltpu.VMEM((N_EXPERTS, 1), f32),
                        pltpu.VMEM((tm, tm), MXU_DTYPE),
                        pltpu.VMEM((2, tm * ROW_CHUNKS, LANES), f32),
                        pltpu.VMEM((SEL_ROWS, tm), jnp.int32),
                        pltpu.SMEM((SEL_ROWS, tm), jnp.int32),
                        pltpu.SemaphoreType.DMA(()),
                        pltpu.SemaphoreType.DMA((2,))],
        compiler_params=pltpu.CompilerParams(
            dimension_semantics=("arbitrary",), vmem_limit_bytes=VMEM_LIMIT_BYTES, has_side_effects=True),
        name="route",
    )(x2d, attn2d, ga, gp, *weights)


def _experts_kernel(blk_ref, bexp_ref, nval_ref, first_ref, next_ref, slot_ref,
                    xs_ref, wgu_ref, bgu_ref, wd_ref, bd_ref,
                    y_ref, wgu_f32, wd_f32, wgu_bf, wd_bf, gu_ref, sems):
    i = pl.program_id(0)
    rows = xs_ref.shape[0] // ROW_CHUNKS

    def weight_copies(expert, slot):
        return (pltpu.make_async_copy(wgu_ref.at[expert], wgu_f32.at[slot], sems.at[slot, 0]),
                pltpu.make_async_copy(wd_ref.at[expert], wd_f32.at[slot], sems.at[slot, 1]))

    @pl.when(i == 0)
    def _():
        for copy in weight_copies(bexp_ref[0], 0):
            copy.start()

    @pl.when(first_ref[i] == 1)
    def _():
        slot = slot_ref[i]

        @pl.when(next_ref[i] >= 0)
        def _():
            for copy in weight_copies(next_ref[i], 1 - slot):
                copy.start()

        for copy in weight_copies(bexp_ref[i], slot):
            copy.wait()
        wgu_bf[...] = wgu_f32[slot].astype(MXU_DTYPE)
        wd_bf[...] = wd_f32[slot].astype(MXU_DTYPE)

    def ffn(n_rows):
        tiles = pl.ds(0, n_rows * ROW_CHUNKS)
        row = lax.broadcasted_iota(jnp.int32, (n_rows, 1), 0)
        x = jnp.where(row < nval_ref[i], _from_token_tiles(xs_ref.at[tiles]), 0.0).astype(MXU_DTYPE)
        gu = gu_ref.at[pl.ds(0, n_rows)]
        gu[...] = _dot(x, wgu_bf[...]) + bgu_ref[0]
        gate = jnp.minimum(gu[:, :D_EXPERT], SWIGLU_LIMIT)
        up = jnp.clip(gu[:, D_EXPERT:], -SWIGLU_LIMIT, SWIGLU_LIMIT)
        act = (up + 1.0) * gate * jax.nn.sigmoid(SWIGLU_ALPHA * gate)
        _to_token_tiles(y_ref.at[tiles], _dot(act.astype(MXU_DTYPE), wd_bf[...]) + bd_ref[0])

    @pl.when(nval_ref[i] > rows // 2)
    def _():
        ffn(rows)

    @pl.when((nval_ref[i] > 0) & (nval_ref[i] <= rows // 2))
    def _():
        ffn(rows // 2)


def _experts(table, xs, w_gate_up, b_gate_up, w_down, b_down):
    rows = EXPERT_ROWS
    nb = table[0].shape[0]
    row_block = pl.BlockSpec((rows * ROW_CHUNKS, LANES), lambda i, blk, *_: (blk[i], 0))
    return pl.pallas_call(
        _experts_kernel,
        grid_spec=pltpu.PrefetchScalarGridSpec(
            num_scalar_prefetch=len(table),
            grid=(nb,),
            in_specs=[
                row_block,
                pl.BlockSpec(memory_space=pl.ANY),
                pl.BlockSpec((1, 1, 2 * D_EXPERT), lambda i, blk, bexp, *_: (bexp[i], 0, 0)),
                pl.BlockSpec(memory_space=pl.ANY),
                pl.BlockSpec((1, 1, D_MODEL), lambda i, blk, bexp, *_: (bexp[i], 0, 0)),
            ],
            out_specs=row_block,
            scratch_shapes=[pltpu.VMEM((2, D_MODEL, 2 * D_EXPERT), jnp.float32),
                            pltpu.VMEM((2, D_EXPERT, D_MODEL), jnp.float32),
                            pltpu.VMEM((D_MODEL, 2 * D_EXPERT), MXU_DTYPE),
                            pltpu.VMEM((D_EXPERT, D_MODEL), MXU_DTYPE),
                            pltpu.VMEM((rows, 2 * D_EXPERT), jnp.float32),
                            pltpu.SemaphoreType.DMA((2, 2))],
        ),
        out_shape=jax.ShapeDtypeStruct(xs.shape, jnp.float32),
        compiler_params=pltpu.CompilerParams(
            dimension_semantics=("arbitrary",), vmem_limit_bytes=VMEM_LIMIT_BYTES),
        name="experts",
    )(*table, xs, w_gate_up, b_gate_up[:, None, :], w_down, b_down[:, None, :])


def _expert_blocks(counts, tokens):
    rows = EXPERT_ROWS
    nb_max = (tokens * TOP_K) // rows + N_EXPERTS
    nblk = (counts + rows - 1) // rows
    ends = jnp.cumsum(nblk)
    starts = ends - nblk
    total = ends[-1]
    step = jnp.arange(nb_max, dtype=jnp.int32)
    i = jnp.minimum(step, total - 1)[:, None]
    owner = ((i >= starts[None, :]) & (i < ends[None, :])).astype(jnp.int32)
    pick = lambda v: jnp.sum(owner * v[None, :], axis=1)
    e = pick(jnp.arange(N_EXPERTS, dtype=jnp.int32))
    j = i[:, 0] - pick(starts)
    active = step < total
    blk = e * (tokens // rows) + j
    nval = jnp.where(active, jnp.clip(pick(counts) - j * rows, 0, rows), 0)
    first = jnp.where(active & (j == 0), 1, 0)
    ids = jnp.arange(N_EXPERTS, dtype=jnp.int32)
    has_rows = nblk > 0
    later = has_rows[None, :] & (ids[None, :] > ids[:, None])
    nxt = jnp.min(jnp.where(later, ids[None, :], N_EXPERTS), axis=1)
    nxt = jnp.where(nxt == N_EXPERTS, -1, nxt)
    slot = (jnp.cumsum(has_rows.astype(jnp.int32)) - 1) % 2
    table = (blk, e, nval, first, pick(nxt), pick(slot))
    return tuple(col.astype(jnp.int32) for col in table)


def _combine_kernel(tokens, dest_ref, gate_ref, x1_ref, p_ref, y_ref, gple_ref, wpg_ref, wple_ref,
                    o_ref, buf, sems):
    tm = x1_ref.shape[0]
    i = pl.program_id(0)
    n = pl.num_programs(0)

    def issue(tile, slot):
        base = tile * tm

        def body(t, carry):
            for k in range(TOP_K):
                d = dest_ref[k * tokens + base + t]
                pltpu.make_async_copy(_token_tile(y_ref, d), _token_tile(buf.at[slot, k], t),
                                      sems.at[slot]).start(priority=k % 2)
            return carry

        lax.fori_loop(0, tm, body, 0, unroll=8)

    @pl.when(i == 0)
    def _():
        issue(0, 0)

    @pl.when(i + 1 < n)
    def _():
        issue(i + 1, (i + 1) % 2)

    slot = i % 2
    for k in range(TOP_K):
        _row_copy_wait(y_ref.at[pl.ds(0, tm * ROW_CHUNKS)], buf.at[slot, k], sems.at[slot])

    gates = gate_ref[...]
    moe = gates[:, 0:1] * _from_token_tiles(buf.at[slot, 0])
    for k in range(1, TOP_K):
        moe = moe + gates[:, k:k + 1] * _from_token_tiles(buf.at[slot, k])
    x2 = x1_ref[...] + moe
    hp = (x2 * _rms_scale(x2, D_MODEL) * gple_ref[...]).astype(MXU_DTYPE)
    ple_gate = jax.nn.sigmoid(_dot(hp, wpg_ref[...]))
    o_ref[...] = x2 + ple_gate * _dot(p_ref[...].astype(MXU_DTYPE), wple_ref[...])


def _combine(dest_flat, gates, x1, p2d, y, g_ple, w_ple_gate, w_ple):
    t = x1.shape[0]
    tm = TOKEN_TILE
    weights = (g_ple[None, :], w_ple_gate.astype(MXU_DTYPE), w_ple.astype(MXU_DTYPE))
    return pl.pallas_call(
        functools.partial(_combine_kernel, t),
        grid_spec=pltpu.PrefetchScalarGridSpec(
            num_scalar_prefetch=1,
            grid=(t // tm,),
            in_specs=[
                pl.BlockSpec((tm, TOP_K), lambda i, dest: (i, 0)),
                pl.BlockSpec((tm, D_MODEL), lambda i, dest: (i, 0)),
                pl.BlockSpec((tm, PLE_DIM), lambda i, dest: (i, 0)),
                pl.BlockSpec(memory_space=pl.ANY),
            ] + [pl.BlockSpec(w.shape, lambda i, dest, _nd=w.ndim: (0,) * _nd) for w in weights],
            out_specs=pl.BlockSpec((tm, D_MODEL), lambda i, dest: (i, 0)),
            scratch_shapes=[pltpu.VMEM((2, TOP_K, tm * ROW_CHUNKS, LANES), jnp.float32),
                            pltpu.SemaphoreType.DMA((2,))],
        ),
        out_shape=jax.ShapeDtypeStruct((t, D_MODEL), jnp.float32),
        compiler_params=pltpu.CompilerParams(
            dimension_semantics=("arbitrary",), vmem_limit_bytes=VMEM_LIMIT_BYTES),
        name="combine",
    )(dest_flat, gates, x1, p2d, y, *weights)


def kernel(x, p, positions, g_mix, w_in, g_q_lat, w_q_b, g_kv_lat, w_kv_b, g_q_head, g_k_nope, g_k_rope, w_pool, pool_scale, w_attn_out, w_pool_out, w_o, g_ffn, w_router, b_router, w_gate_up, b_gate_up, w_down, b_down, g_ple, w_ple_gate, w_ple):
    batch, seq, _ = x.shape
    tokens = batch * seq
    depth = g_mix.shape[0]
    assert seq % TOKEN_TILE == 0 and seq % ATTN_TILE == 0 and tokens % EXPERT_ROWS == 0
    x2d = x.reshape(tokens, D_MODEL)
    pos2d = positions.astype(jnp.float32).reshape(1, tokens)
    for i in range(depth):
        weights = _premix_weights(g_mix[i], w_in[i], g_q_lat[i], w_q_b[i], g_kv_lat[i], w_kv_b[i],
                                  g_q_head[i], g_k_nope[i], g_k_rope[i], w_pool[i], pool_scale[i],
                                  w_pool_out[i])
        q, kt, v, ga, gp = _premix(x2d, pos2d, weights, batch, seq)
        attn = _attention(q, kt, v).reshape(tokens, N_HEADS * V_HEAD)
        x1, dest, gates, counts, xs = _route(x2d, attn, ga, gp, w_attn_out[i], w_o[i], g_ffn[i],
                                             w_router[i], b_router[i])
        table = _expert_blocks(counts[:, 0], tokens)
        y = _experts(table, xs, w_gate_up[i], b_gate_up[i], w_down[i], b_down[i])
        x2d = _combine(dest[:TOP_K].reshape(-1), gates[:TOP_K].T, x1, p[i].reshape(tokens, PLE_DIM), y,
                       g_ple[i], w_ple_gate[i], w_ple[i])
    return x2d.reshape(batch, seq, D_MODEL)
```

```python
import functools
import math

import jax
import jax.numpy as jnp
from jax import lax
from jax.experimental import pallas as pl
from jax.experimental.pallas import tpu as pltpu

D_MODEL = 1024
N_HEADS = 8
QK_NOPE = 64
QK_ROPE = 32
QK_HEAD = QK_NOPE + QK_ROPE
V_HEAD = 64
Q_LORA = 256
KV_LORA = 128
ROPE_THETA = 10000.0
EPS = 1e-6
POOL_WINDOWS = (2, 4, 8, 16)
POOL_GROUPS = 4
POOL_WIDTH = 512
POOL_GROUP_DIM = POOL_WIDTH // POOL_GROUPS
POOL_HALO = 16
N_EXPERTS = 32
TOP_K = 4
D_EXPERT = 1024
SWIGLU_ALPHA = 1.702
SWIGLU_LIMIT = 7.0
PLE_DIM = 256

LANES = 128
HEAD_SLAB = LANES
ROW_CHUNKS = D_MODEL // LANES
VMEM_LIMIT_BYTES = 56 * 1024 * 1024

MXU_DTYPE = jnp.bfloat16
NEG_BIG = -1e30
Q_SCALE = math.log2(math.e) / math.sqrt(QK_HEAD)

TOKEN_TILE = 512
ATTN_TILE = 512
ATTN_HEADS = 4
EXPERT_ROWS = 512
ISSUE_UNROLL = 8
COMBINE_GROUP = 16


def _dot(a, b):
    return jnp.dot(a, b, preferred_element_type=jnp.float32)


def _rms_scale(v, width):
    return lax.rsqrt(jnp.sum(v * v, axis=-1, keepdims=True) * (1.0 / width) + EPS)


_C_QLAT = 0
_C_KVLAT = _C_QLAT + Q_LORA
_C_KR = _C_KVLAT + KV_LORA
_C_KRR = _C_KR + HEAD_SLAB
_C_POOL = _C_KRR + HEAD_SLAB
_C_GA = _C_POOL + POOL_WIDTH
_C_GP = _C_GA + D_MODEL
_C_END = _C_GP + D_MODEL


def _premix_kernel(tiles_per_seq, x_ref, pos_ref, gmix_ref, w1_ref, gql_ref, wqs_ref, wqr_ref,
                   gkvl_ref, wks_ref, wv_ref, gqs_ref, gqr_ref, gks_ref, gkrs_ref, gkrr_ref,
                   invf_ref, wpool_ref, pscale_ref, wpo_ref,
                   q_ref, kt_ref, v_ref, ga_ref, gp_ref, halo_ref, proj_ref):
    @pl.when(pl.program_id(0) == 0)
    def _():
        halo_ref[...] = jnp.zeros_like(halo_ref)

    xv = x_ref[...]
    h = (xv * _rms_scale(xv, D_MODEL) * gmix_ref[...]).astype(MXU_DTYPE)
    proj_ref[...] = _dot(h, w1_ref[...])
    for section in _premix_sections(tiles_per_seq, proj_ref, pos_ref, gql_ref, wqs_ref, wqr_ref, gkvl_ref,
                                    wks_ref, wv_ref, gqs_ref, gqr_ref, gks_ref, gkrs_ref, gkrr_ref, invf_ref,
                                    wpool_ref, pscale_ref, wpo_ref, q_ref, kt_ref, v_ref, ga_ref, gp_ref,
                                    halo_ref):
        section()


def _premix_sections(tiles_per_seq, proj, pos_ref, gql_ref, wqs_ref, wqr_ref, gkvl_ref, wks_ref, wv_ref,
                     gqs_ref, gqr_ref, gks_ref, gkrs_ref, gkrr_ref, invf_ref, wpool_ref, pscale_ref,
                     wpo_ref, q_ref, kt_ref, v_ref, ga_ref, gp_ref, halo_ref):
    tm = proj.shape[0]
    si = pl.program_id(0) % tiles_per_seq
    shared = {}

    def rope_tables():
        if "cos" not in shared:
            ang = invf_ref[...] * pos_ref[...]
            expand = lambda tab: jnp.tile(tab, (LANES // tab.shape[0], 1)).T
            shared["cos"] = expand(jnp.cos(ang))
            shared["sin"] = expand(jnp.sin(ang))
        return shared["cos"], shared["sin"]

    def q_project():
        q_lat = proj[:, _C_QLAT:_C_QLAT + Q_LORA]
        qln = (q_lat * _rms_scale(q_lat, Q_LORA) * gql_ref[...]).astype(MXU_DTYPE)
        shared["qs"] = _dot(qln, wqs_ref[...])
        shared["qr"] = _dot(qln, wqr_ref[...])

    def q_heads(heads):
        def run():
            cos_t, sin_t = rope_tables()
            lane = lax.broadcasted_iota(jnp.int32, (tm, LANES), 1)
            q_cos = gqs_ref[...] * jnp.where(lane < QK_NOPE, 1.0, cos_t)
            q_sin = gqr_ref[...] * sin_t
            for hd in heads:
                s = shared["qs"][:, hd * HEAD_SLAB:(hd + 1) * HEAD_SLAB]
                r = shared["qr"][:, hd * HEAD_SLAB:(hd + 1) * HEAD_SLAB]
                scale = _rms_scale(s, QK_HEAD) * Q_SCALE
                q_ref[0, hd] = ((s * q_cos + r * q_sin) * scale).astype(q_ref.dtype)
        return run

    def kv_project():
        cos_t, sin_t = rope_tables()
        kv_lat = proj[:, _C_KVLAT:_C_KVLAT + KV_LORA]
        kvn = (kv_lat * _rms_scale(kv_lat, KV_LORA) * gkvl_ref[...]).astype(MXU_DTYPE)
        shared["ks"] = _dot(kvn, wks_ref[...])
        v_ref[0] = _dot(kvn, wv_ref[...]).astype(v_ref.dtype)
        krs = proj[:, _C_KR:_C_KR + HEAD_SLAB]
        krr = proj[:, _C_KRR:_C_KRR + HEAD_SLAB]
        shared["k_rot"] = ((krs * (gkrs_ref[...] * cos_t) + krr * (gkrr_ref[...] * sin_t))
                           * _rms_scale(krs, QK_ROPE))

    def k_heads(heads):
        def run():
            for hd in heads:
                s = shared["ks"][:, hd * HEAD_SLAB:(hd + 1) * HEAD_SLAB]
                kh = s * _rms_scale(s, QK_NOPE) * gks_ref[...] + shared["k_rot"]
                kt_ref[0, hd] = kh.T.astype(kt_ref.dtype)
        return run

    def pool_groups(groups):
        def run():
            t_seq = si * tm + lax.broadcasted_iota(jnp.int32, (tm, 1), 0)
            for g in groups:
                w = POOL_WINDOWS[g]
                cols = slice(g * POOL_GROUP_DIM, (g + 1) * POOL_GROUP_DIM)
                u = proj[:, _C_POOL + cols.start:_C_POOL + cols.stop]
                history = jnp.where(si == 0, 0.0, halo_ref[:, cols])
                acc = jnp.concatenate([history, u], axis=0)
                halo_ref[:, cols] = u[tm - POOL_HALO:, :]
                shift = 1
                while shift < w:
                    acc = acc + pltpu.roll(acc, shift, axis=0)
                    shift *= 2
                count = jnp.minimum(t_seq + 1, w).astype(jnp.float32)
                pooled = acc[POOL_HALO:, :] / count - u
                shared[("mixed", g)] = _dot(pooled.astype(MXU_DTYPE), wpool_ref[g])
        return run

    def pool_out():
        mixed = jnp.concatenate([shared[("mixed", g)] for g in range(POOL_GROUPS)], axis=1) * pscale_ref[...]
        shared["y_pool"] = _dot(mixed.astype(MXU_DTYPE), wpo_ref[...])

    def gate_attn():
        ga_ref[...] = jax.nn.sigmoid(proj[:, _C_GA:_C_GA + D_MODEL]).astype(ga_ref.dtype)

    def gate_pool():
        gp_ref[...] = (jax.nn.sigmoid(proj[:, _C_GP:_C_GP + D_MODEL]) * shared["y_pool"]).astype(gp_ref.dtype)

    half = N_HEADS // 2
    return [q_project, kv_project, q_heads(range(half)), q_heads(range(half, N_HEADS)),
            k_heads(range(half)), k_heads(range(half, N_HEADS)),
            pool_groups((0, 1)), pool_groups((2, 3)), pool_out, gate_attn, gate_pool]


def _premix_weights(g_mix, w_in, g_q_lat, w_q_b, g_kv_lat, w_kv_b, g_q_head, g_k_nope, g_k_rope,
                    w_pool, pool_scale, w_pool_out):
    f32 = jnp.float32
    half = QK_ROPE // 2
    w_q_lat = w_in[:, :Q_LORA]
    w_kv_lat = w_in[:, Q_LORA:Q_LORA + KV_LORA]
    w_kr = w_in[:, Q_LORA + KV_LORA:Q_LORA + KV_LORA + QK_ROPE]
    off = Q_LORA + KV_LORA + QK_ROPE
    w_pool_in = w_in[:, off:off + POOL_WIDTH]
    w_ga = w_in[:, off + POOL_WIDTH:off + POOL_WIDTH + D_MODEL]
    w_gp = w_in[:, off + POOL_WIDTH + D_MODEL:]

    kr_s = jnp.zeros((D_MODEL, HEAD_SLAB), f32).at[:, QK_NOPE:QK_NOPE + QK_ROPE].set(w_kr)
    kr_r = jnp.zeros((D_MODEL, HEAD_SLAB), f32)
    kr_r = kr_r.at[:, QK_NOPE:QK_NOPE + half].set(-w_kr[:, half:])
    kr_r = kr_r.at[:, QK_NOPE + half:QK_NOPE + QK_ROPE].set(w_kr[:, :half])
    w1 = jnp.concatenate([w_q_lat, w_kv_lat, kr_s, kr_r, w_pool_in, w_ga, w_gp], axis=1)

    wq = w_q_b.reshape(Q_LORA, N_HEADS, QK_HEAD)
    wq_s = jnp.zeros((Q_LORA, N_HEADS, HEAD_SLAB), f32).at[:, :, :QK_HEAD].set(wq)
    wq_r = jnp.zeros((Q_LORA, N_HEADS, HEAD_SLAB), f32)
    wq_r = wq_r.at[:, :, QK_NOPE:QK_NOPE + half].set(-wq[:, :, QK_NOPE + half:])
    wq_r = wq_r.at[:, :, QK_NOPE + half:QK_HEAD].set(wq[:, :, QK_NOPE:QK_NOPE + half])
    wkv = w_kv_b.reshape(KV_LORA, N_HEADS, QK_NOPE + V_HEAD)
    wk_s = jnp.zeros((KV_LORA, N_HEADS, HEAD_SLAB), f32).at[:, :, :QK_NOPE].set(wkv[:, :, :QK_NOPE])
    wv = wkv[:, :, QK_NOPE:].reshape(KV_LORA, N_HEADS * V_HEAD)

    def slab(vals, start):
        return jnp.zeros((1, HEAD_SLAB), f32).at[0, start:start + vals.shape[0]].set(vals)

    gq_s = slab(g_q_head, 0)
    gq_r = slab(jnp.concatenate([g_q_head[QK_NOPE + half:], g_q_head[QK_NOPE:QK_NOPE + half]]), QK_NOPE)
    gk_s = slab(g_k_nope, 0)
    gkr_s = slab(g_k_rope, QK_NOPE)
    gkr_r = slab(jnp.concatenate([g_k_rope[half:], g_k_rope[:half]]), QK_NOPE)
    inv_freq = ROPE_THETA ** (-jnp.arange(0, QK_ROPE, 2, dtype=f32) / QK_ROPE)
    invf = inv_freq[:, None]
    bf = MXU_DTYPE
    return (g_mix[None, :], w1.astype(bf), g_q_lat[None, :],
            wq_s.reshape(Q_LORA, -1).astype(bf), wq_r.reshape(Q_LORA, -1).astype(bf),
            g_kv_lat[None, :], wk_s.reshape(KV_LORA, -1).astype(bf), wv.astype(bf),
            gq_s, gq_r, gk_s, gkr_s, gkr_r, invf, w_pool.astype(bf), pool_scale[None, :],
            w_pool_out.astype(bf))


def _full_spec(arr):
    nd = arr.ndim
    return pl.BlockSpec(arr.shape, lambda i, _nd=nd: (0,) * _nd)


def _premix(x2d, pos2d, weights, batch, seq):
    t = x2d.shape[0]
    tm = TOKEN_TILE
    tiles_per_seq = seq // tm
    in_specs = [pl.BlockSpec((tm, D_MODEL), lambda i: (i, 0)),
                pl.BlockSpec((1, tm), lambda i: (0, i))] + [_full_spec(w) for w in weights]
    out_shape = (
        jax.ShapeDtypeStruct((batch, N_HEADS, seq, HEAD_SLAB), MXU_DTYPE),
        jax.ShapeDtypeStruct((batch, N_HEADS, HEAD_SLAB, seq), MXU_DTYPE),
        jax.ShapeDtypeStruct((batch, seq, N_HEADS * V_HEAD), MXU_DTYPE),
        jax.ShapeDtypeStruct((t, D_MODEL), MXU_DTYPE),
        jax.ShapeDtypeStruct((t, D_MODEL), MXU_DTYPE),
    )
    out_specs = (
        pl.BlockSpec((1, N_HEADS, tm, HEAD_SLAB), lambda i: (i // tiles_per_seq, 0, i % tiles_per_seq, 0)),
        pl.BlockSpec((1, N_HEADS, HEAD_SLAB, tm), lambda i: (i // tiles_per_seq, 0, 0, i % tiles_per_seq)),
        pl.BlockSpec((1, tm, N_HEADS * V_HEAD), lambda i: (i // tiles_per_seq, i % tiles_per_seq, 0)),
        pl.BlockSpec((tm, D_MODEL), lambda i: (i, 0)),
        pl.BlockSpec((tm, D_MODEL), lambda i: (i, 0)),
    )
    return pl.pallas_call(
        functools.partial(_premix_kernel, tiles_per_seq),
        grid=(t // tm,),
        in_specs=in_specs,
        out_specs=out_specs,
        out_shape=out_shape,
        scratch_shapes=[pltpu.VMEM((POOL_HALO, POOL_WIDTH), jnp.float32),
                        pltpu.VMEM((tm, _C_END), jnp.float32)],
        compiler_params=pltpu.CompilerParams(
            dimension_semantics=("arbitrary",), vmem_limit_bytes=VMEM_LIMIT_BYTES),
        name="premix",
    )(x2d, pos2d, *weights)


def _attention_kernel(q_ref, kt_ref, v_ref, o_ref, vsel_ref, s_ref, m_ref, acc_ref):
    tq = q_ref.shape[2]
    tk = tq
    qi = pl.program_id(2)
    heads = range(q_ref.shape[1])

    ones_lane = (V_HEAD, 0)

    @pl.when(qi == 0)
    def _():
        for hh in heads:
            v = v_ref[0, :, (hh // 2) * HEAD_SLAB:(hh // 2 + 1) * HEAD_SLAB].astype(jnp.float32)
            lane = lax.broadcasted_iota(jnp.int32, v.shape, 1)
            own = (lane < V_HEAD) if hh % 2 == 0 else (lane >= V_HEAD)
            pad = (lane == ones_lane[hh % 2]).astype(jnp.float32)
            vsel_ref[hh] = jnp.where(own, v, pad).astype(vsel_ref.dtype)

    def scores(blk, slot):
        start = pl.multiple_of(blk * tk, tk)
        for hh in heads:
            s_ref[slot, hh] = _dot(q_ref[0, hh], kt_ref[0, hh, :, pl.ds(start, tk)])

    def softmax_pv(blk, slot, masked):
        start = pl.multiple_of(blk * tk, tk)
        for hh in heads:
            if masked:
                row = lax.broadcasted_iota(jnp.int32, (tq, tk), 0)
                col = lax.broadcasted_iota(jnp.int32, (tq, tk), 1)
                s_ref[slot, hh] = jnp.where(row >= col, s_ref[slot, hh], NEG_BIG)
            m_old = m_ref[hh]
            m_new = jnp.maximum(m_old, jnp.max(s_ref[slot, hh], axis=-1, keepdims=True))
            p = jnp.exp2(s_ref[slot, hh] - m_new).astype(MXU_DTYPE)
            acc_ref[hh] = jnp.exp2(m_old - m_new) * acc_ref[hh] + _dot(p, vsel_ref[hh, pl.ds(start, tk), :])
            m_ref[hh] = m_new

    m_ref[...] = jnp.full(m_ref.shape, NEG_BIG, jnp.float32)
    acc_ref[...] = jnp.zeros(acc_ref.shape, jnp.float32)
    scores(0, 0)

    def two_blocks(jj, carry):
        scores(2 * jj + 1, 1)
        softmax_pv(2 * jj, 0, False)
        scores(2 * jj + 2, 0)
        softmax_pv(2 * jj + 1, 1, False)
        return carry

    lax.fori_loop(0, qi // 2, two_blocks, 0)

    @pl.when(qi % 2 == 0)
    def _():
        softmax_pv(qi, 0, True)

    @pl.when(qi % 2 == 1)
    def _():
        scores(qi, 1)
        softmax_pv(qi - 1, 0, False)
        softmax_pv(qi, 1, True)

    lane = lax.broadcasted_iota(jnp.int32, (tq, HEAD_SLAB), 1)
    for pair in range(len(heads) // 2):
        acc_a = acc_ref[2 * pair]
        acc_b = acc_ref[2 * pair + 1]
        out_a = acc_a / acc_a[:, ones_lane[0]:ones_lane[0] + 1]
        out_b = acc_b / acc_b[:, ones_lane[1]:ones_lane[1] + 1]
        o_ref[0, :, pair * HEAD_SLAB:(pair + 1) * HEAD_SLAB] = jnp.where(lane < V_HEAD, out_a, out_b).astype(o_ref.dtype)


def _attention(q, kt, v):
    batch, _, seq, _ = q.shape
    tq = ATTN_TILE
    hg = ATTN_HEADS
    return pl.pallas_call(
        _attention_kernel,
        grid=(batch, N_HEADS // hg, seq // tq),
        in_specs=[
            pl.BlockSpec((1, hg, tq, HEAD_SLAB), lambda b, g, i: (b, g, i, 0)),
            pl.BlockSpec((1, hg, HEAD_SLAB, seq), lambda b, g, i: (b, g, 0, 0)),
            pl.BlockSpec((1, seq, hg * V_HEAD), lambda b, g, i: (b, 0, g)),
        ],
        out_specs=pl.BlockSpec((1, tq, hg * V_HEAD), lambda b, g, i: (b, i, g)),
        out_shape=jax.ShapeDtypeStruct((batch, seq, N_HEADS * V_HEAD), MXU_DTYPE),
        scratch_shapes=[pltpu.VMEM((hg, seq, HEAD_SLAB), MXU_DTYPE),
                        pltpu.VMEM((2, hg, tq, tq), jnp.float32),
                        pltpu.VMEM((hg, tq, 1), jnp.float32),
                        pltpu.VMEM((hg, tq, HEAD_SLAB), jnp.float32)],
        compiler_params=pltpu.CompilerParams(
            dimension_semantics=("arbitrary", "arbitrary", "arbitrary"), vmem_limit_bytes=VMEM_LIMIT_BYTES),
        name="attention",
    )(q, kt, v)


def _row_copy_wait(src_ref, dst_ref, sem):
    pltpu.make_async_copy(src_ref, dst_ref, sem).wait()


def _to_token_tiles(dst_ref, rows2d):
    rows = rows2d.shape[0]
    for c in range(ROW_CHUNKS):
        dst_ref[pl.ds(c, rows, stride=ROW_CHUNKS), :] = rows2d[:, c * LANES:(c + 1) * LANES]


def _from_token_tiles(src_ref):
    rows = src_ref.shape[0] // ROW_CHUNKS
    return jnp.concatenate([src_ref[pl.ds(c, rows, stride=ROW_CHUNKS), :] for c in range(ROW_CHUNKS)], axis=1)


def _token_tile(ref, index):
    return ref.at[pl.ds(pl.multiple_of(index * ROW_CHUNKS, ROW_CHUNKS), ROW_CHUNKS)]


SEL_ROWS = 8


def _route_kernel(tokens, x_ref, at_ref, ga_ref, gp_ref, wao_ref, wo_ref, gffn_ref, wrh_ref, wrl_ref, br_ref,
                  x1_ref, dest_ref, gate_ref, cnt_ref, xs_ref,
                  carry_ref, upper_ref, tiles_ref, stage_ref, dest_smem, stage_sem, sems):
    tm = x_ref.shape[0]
    i = pl.program_id(0)
    n = pl.num_programs(0)

    @pl.when(i == 0)
    def _():
        carry_ref[...] = jnp.zeros_like(carry_ref)
        a_i = lax.broadcasted_iota(jnp.int32, (tm, tm), 0)
        b_i = lax.broadcasted_iota(jnp.int32, (tm, tm), 1)
        upper_ref[...] = (a_i < b_i).astype(upper_ref.dtype)

    y_attn = _dot(at_ref[...], wao_ref[...])
    merged = ga_ref[...].astype(jnp.float32) * y_attn + gp_ref[...].astype(jnp.float32)
    x1_ref[...] = x_ref[...] + _dot(merged.astype(MXU_DTYPE), wo_ref[...])
    x1 = x1_ref[...]
    h2 = x1 * _rms_scale(x1, D_MODEL) * gffn_ref[...]
    slot = i % 2
    _to_token_tiles(tiles_ref.at[slot], h2)

    h_hi = h2.astype(MXU_DTYPE)
    h_lo = (h2 - h_hi.astype(jnp.float32)).astype(MXU_DTYPE)
    logits = _dot(h_hi, wrh_ref[...]) + (_dot(h_lo, wrh_ref[...]) + _dot(h_hi, wrl_ref[...])) + br_ref[...]
    work = logits.T[:N_EXPERTS]

    expert = lax.broadcasted_iota(jnp.int32, (N_EXPERTS, tm), 0)
    vals, idxs = [], []
    for _ in range(TOP_K):
        mx = jnp.max(work, axis=0, keepdims=True)
        ix = jnp.min(jnp.where(work == mx, expert, N_EXPERTS), axis=0, keepdims=True)
        vals.append(mx)
        idxs.append(ix)
        work = jnp.where(expert == ix, -jnp.inf, work)
    exps = [jnp.exp(v - vals[0]) for v in vals]
    denom = exps[0] + exps[1] + exps[2] + exps[3]

    onehot = jnp.zeros((N_EXPERTS, tm), jnp.float32)
    for ix in idxs:
        onehot = onehot + (expert == ix).astype(jnp.float32)
    before = carry_ref[...] + _dot(onehot.astype(MXU_DTYPE), upper_ref[...])
    carry_ref[...] = carry_ref[...] + jnp.sum(onehot, axis=1, keepdims=True)
    cnt_ref[...] = carry_ref[...].astype(jnp.int32)

    sel = lax.broadcasted_iota(jnp.int32, (SEL_ROWS, tm), 0)
    dest = jnp.zeros((SEL_ROWS, tm), jnp.int32)
    gate = jnp.zeros((SEL_ROWS, tm), jnp.float32)
    for k, ix in enumerate(idxs):
        rank = jnp.sum(jnp.where(expert == ix, before, 0.0), axis=0, keepdims=True).astype(jnp.int32)
        dest = jnp.where(sel == k, ix * tokens + rank, dest)
        gate = jnp.where(sel == k, exps[k] / denom, gate)
    dest_ref[...] = dest
    gate_ref[...] = gate

    stage_ref[...] = dest
    to_smem = pltpu.make_async_copy(stage_ref, dest_smem, stage_sem)
    to_smem.start()
    to_smem.wait()

    def issue(t, carry):
        for k in range(TOP_K):
            pltpu.make_async_copy(_token_tile(tiles_ref.at[slot], t), _token_tile(xs_ref, dest_smem[k, t]),
                                  sems.at[slot]).start(priority=k % 2)
        return carry

    lax.fori_loop(0, tm, issue, 0, unroll=ISSUE_UNROLL)

    def wait_tile(s):
        for _ in range(TOP_K):
            _row_copy_wait(tiles_ref.at[s], xs_ref.at[pl.ds(0, tm * ROW_CHUNKS)], sems.at[s])

    @pl.when(i > 0)
    def _():
        wait_tile(1 - slot)

    @pl.when(i == n - 1)
    def _():
        wait_tile(slot)


def _route(x2d, attn2d, ga, gp, w_attn_out, w_o, g_ffn, w_router, b_router):
    t = x2d.shape[0]
    tm = TOKEN_TILE
    f32 = jnp.float32
    wr = jnp.zeros((D_MODEL, LANES), f32).at[:, :N_EXPERTS].set(w_router)
    wr_hi = wr.astype(MXU_DTYPE)
    wr_lo = (wr - wr_hi.astype(f32)).astype(MXU_DTYPE)
    br = jnp.full((1, LANES), NEG_BIG, f32).at[0, :N_EXPERTS].set(b_router)
    weights = (w_attn_out.astype(MXU_DTYPE), w_o.astype(MXU_DTYPE), g_ffn[None, :], wr_hi, wr_lo, br)
    row_spec = lambda width: pl.BlockSpec((tm, width), lambda i: (i, 0))
    sel_spec = pl.BlockSpec((SEL_ROWS, tm), lambda i: (0, i))
    return pl.pallas_call(
        functools.partial(_route_kernel, t),
        grid=(t // tm,),
        in_specs=[row_spec(D_MODEL), row_spec(N_HEADS * V_HEAD), row_spec(D_MODEL), row_spec(D_MODEL)]
        + [_full_spec(w) for w in weights],
        out_specs=(row_spec(D_MODEL), sel_spec, sel_spec,
                   pl.BlockSpec((N_EXPERTS, 1), lambda i: (0, 0)),
                   pl.BlockSpec(memory_space=pl.ANY)),
        out_shape=(
            jax.ShapeDtypeStruct((t, D_MODEL), f32),
            jax.ShapeDtypeStruct((SEL_ROWS, t), jnp.int32),
            jax.ShapeDtypeStruct((SEL_ROWS, t), f32),
            jax.ShapeDtypeStruct((N_EXPERTS, 1), jnp.int32),
            jax.ShapeDtypeStruct((N_EXPERTS * t * ROW_CHUNKS, LANES), f32),
        ),
        scratch_shapes=[pltpu.VMEM((N_EXPERTS, 1), f32),
                        pltpu.VMEM((tm, tm), MXU_DTYPE),
                        pltpu.VMEM((2, tm * ROW_CHUNKS, LANES), f32),
                        pltpu.VMEM((SEL_ROWS, tm), jnp.int32),
                        pltpu.SMEM((SEL_ROWS, tm), jnp.int32),
                        pltpu.SemaphoreType.DMA(()),
                        pltpu.SemaphoreType.DMA((2,))],
        compiler_params=pltpu.CompilerParams(
            dimension_semantics=("arbitrary",), vmem_limit_bytes=VMEM_LIMIT_BYTES, has_side_effects=True),
        name="route",
    )(x2d, attn2d, ga, gp, *weights)


def _experts_kernel(blk_ref, bexp_ref, nval_ref, first_ref, next_ref, slot_ref,
                    xs_ref, wgu_ref, bgu_ref, wd_ref, bd_ref,
                    y_ref, wgu_f32, wd_f32, wgu_bf, wd_bf, gu_ref, sems):
    i = pl.program_id(0)
    rows = xs_ref.shape[0] // ROW_CHUNKS

    def weight_copies(expert, slot):
        return (pltpu.make_async_copy(wgu_ref.at[expert], wgu_f32.at[slot], sems.at[slot, 0]),
                pltpu.make_async_copy(wd_ref.at[expert], wd_f32.at[slot], sems.at[slot, 1]))

    @pl.when(i == 0)
    def _():
        for copy in weight_copies(bexp_ref[0], 0):
            copy.start()

    @pl.when(first_ref[i] == 1)
    def _():
        slot = slot_ref[i]

        @pl.when(next_ref[i] >= 0)
        def _():
            for copy in weight_copies(next_ref[i], 1 - slot):
                copy.start()

        for copy in weight_copies(bexp_ref[i], slot):
            copy.wait()
        wgu_bf[...] = wgu_f32[slot].astype(MXU_DTYPE)
        wd_bf[...] = wd_f32[slot].astype(MXU_DTYPE)

    def ffn(n_rows):
        tiles = pl.ds(0, n_rows * ROW_CHUNKS)
        row = lax.broadcasted_iota(jnp.int32, (n_rows, 1), 0)
        x = jnp.where(row < nval_ref[i], _from_token_tiles(xs_ref.at[tiles]), 0.0).astype(MXU_DTYPE)
        gu = gu_ref.at[pl.ds(0, n_rows)]
        gu[...] = _dot(x, wgu_bf[...]) + bgu_ref[0]
        gate = jnp.minimum(gu[:, :D_EXPERT], SWIGLU_LIMIT)
        up = jnp.clip(gu[:, D_EXPERT:], -SWIGLU_LIMIT, SWIGLU_LIMIT)
        act = (up + 1.0) * gate * jax.nn.sigmoid(SWIGLU_ALPHA * gate)
        _to_token_tiles(y_ref.at[tiles], _dot(act.astype(MXU_DTYPE), wd_bf[...]) + bd_ref[0])

    @pl.when(nval_ref[i] > rows // 2)
    def _():
        ffn(rows)

    @pl.when((nval_ref[i] > 0) & (nval_ref[i] <= rows // 2))
    def _():
        ffn(rows // 2)


def _experts(table, xs, w_gate_up, b_gate_up, w_down, b_down):
    rows = EXPERT_ROWS
    nb = table[0].shape[0]
    row_block = pl.BlockSpec((rows * ROW_CHUNKS, LANES), lambda i, blk, *_: (blk[i], 0))
    return pl.pallas_call(
        _experts_kernel,
        grid_spec=pltpu.PrefetchScalarGridSpec(
            num_scalar_prefetch=len(table),
            grid=(nb,),
            in_specs=[
                row_block,
                pl.BlockSpec(memory_space=pl.ANY),
                pl.BlockSpec((1, 1, 2 * D_EXPERT), lambda i, blk, bexp, *_: (bexp[i], 0, 0)),
                pl.BlockSpec(memory_space=pl.ANY),
                pl.BlockSpec((1, 1, D_MODEL), lambda i, blk, bexp, *_: (bexp[i], 0, 0)),
            ],
            out_specs=row_block,
            scratch_shapes=[pltpu.VMEM((2, D_MODEL, 2 * D_EXPERT), jnp.float32),
                            pltpu.VMEM((2, D_EXPERT, D_MODEL), jnp.float32),
                            pltpu.VMEM((D_MODEL, 2 * D_EXPERT), MXU_DTYPE),
                            pltpu.VMEM((D_EXPERT, D_MODEL), MXU_DTYPE),
                            pltpu.VMEM((rows, 2 * D_EXPERT), jnp.float32),
                            pltpu.SemaphoreType.DMA((2, 2))],
        ),
        out_shape=jax.ShapeDtypeStruct(xs.shape, jnp.float32),
        compiler_params=pltpu.CompilerParams(
            dimension_semantics=("arbitrary",), vmem_limit_bytes=VMEM_LIMIT_BYTES),
        name="experts",
    )(*table, xs, w_gate_up, b_gate_up[:, None, :], w_down, b_down[:, None, :])


def _expert_blocks(counts, tokens):
    rows = EXPERT_ROWS
    nb_max = (tokens * TOP_K) // rows + N_EXPERTS
    nblk = (counts + rows - 1) // rows
    ends = jnp.cumsum(nblk)
    starts = ends - nblk
    total = ends[-1]
    step = jnp.arange(nb_max, dtype=jnp.int32)
    i = jnp.minimum(step, total - 1)[:, None]
    owner = ((i >= starts[None, :]) & (i < ends[None, :])).astype(jnp.int32)
    pick = lambda v: jnp.sum(owner * v[None, :], axis=1)
    e = pick(jnp.arange(N_EXPERTS, dtype=jnp.int32))
    j = i[:, 0] - pick(starts)
    active = step < total
    blk = e * (tokens // rows) + j
    nval = jnp.where(active, jnp.clip(pick(counts) - j * rows, 0, rows), 0)
    first = jnp.where(active & (j == 0), 1, 0)
    ids = jnp.arange(N_EXPERTS, dtype=jnp.int32)
    has_rows = nblk > 0
    later = has_rows[None, :] & (ids[None, :] > ids[:, None])
    nxt = jnp.min(jnp.where(later, ids[None, :], N_EXPERTS), axis=1)
    nxt = jnp.where(nxt == N_EXPERTS, -1, nxt)
    slot = (jnp.cumsum(has_rows.astype(jnp.int32)) - 1) % 2
    table = (blk, e, nval, first, pick(nxt), pick(slot))
    return tuple(col.astype(jnp.int32) for col in table)


def _combine_kernel(tokens, dest_ref, gate_ref, x1_ref, p_ref, y_ref, gple_ref, wpg_ref, wple_ref,
                    o_ref, buf, sems):
    tm = x1_ref.shape[0]
    group = COMBINE_GROUP
    i = pl.program_id(0)
    n = pl.num_programs(0)
    slot = i % 2

    def issue(tile, to_slot, t0):
        for t in range(group):
            for k in range(TOP_K):
                d = dest_ref[k * tokens + tile * tm + t0 + t]
                pltpu.make_async_copy(_token_tile(y_ref, d), _token_tile(buf.at[to_slot, k], t0 + t),
                                      sems.at[to_slot]).start(priority=k % 2)

    def weighted_sum(t0):
        rows = pl.ds(t0, group)
        gates = gate_ref[rows, :]
        for c in range(ROW_CHUNKS):
            cols = slice(c * LANES, (c + 1) * LANES)
            acc = x1_ref[rows, cols]
            for k in range(TOP_K):
                acc = acc + gates[:, k:k + 1] * buf[slot, k, pl.ds(t0 * ROW_CHUNKS + c, group, stride=ROW_CHUNKS), :]
            o_ref[rows, cols] = acc

    @pl.when(i == 0)
    def _():
        lax.fori_loop(0, tm // group, lambda g, c: (issue(0, 0, g * group), c)[1], 0)

    for k in range(TOP_K):
        _row_copy_wait(y_ref.at[pl.ds(0, tm * ROW_CHUNKS)], buf.at[slot, k], sems.at[slot])

    @pl.when(i + 1 < n)
    def _():
        def trip(g, carry):
            t0 = pl.multiple_of(g * group, group)
            issue(i + 1, 1 - slot, t0)
            weighted_sum(t0)
            return carry

        lax.fori_loop(0, tm // group, trip, 0)

    @pl.when(i + 1 == n)
    def _():
        lax.fori_loop(0, tm // group, lambda g, c: (weighted_sum(pl.multiple_of(g * group, group)), c)[1], 0)

    x2 = o_ref[...]
    hp = (x2 * _rms_scale(x2, D_MODEL) * gple_ref[...]).astype(MXU_DTYPE)
    ple_gate = jax.nn.sigmoid(_dot(hp, wpg_ref[...]))
    o_ref[...] = x2 + ple_gate * _dot(p_ref[...].astype(MXU_DTYPE), wple_ref[...])


def _combine(dest_flat, gates, x1, p2d, y, g_ple, w_ple_gate, w_ple):
    t = x1.shape[0]
    tm = TOKEN_TILE
    weights = (g_ple[None, :], w_ple_gate.astype(MXU_DTYPE), w_ple.astype(MXU_DTYPE))
    return pl.pallas_call(
        functools.partial(_combine_kernel, t),
        grid_spec=pltpu.PrefetchScalarGridSpec(
            num_scalar_prefetch=1,
            grid=(t // tm,),
            in_specs=[
                pl.BlockSpec((tm, TOP_K), lambda i, dest: (i, 0)),
                pl.BlockSpec((tm, D_MODEL), lambda i, dest: (i, 0)),
                pl.BlockSpec((tm, PLE_DIM), lambda i, dest: (i, 0)),
                pl.BlockSpec(memory_space=pl.ANY),
            ] + [pl.BlockSpec(w.shape, lambda i, dest, _nd=w.ndim: (0,) * _nd) for w in weights],
            out_specs=pl.BlockSpec((tm, D_MODEL), lambda i, dest: (i, 0)),
            scratch_shapes=[pltpu.VMEM((2, TOP_K, tm * ROW_CHUNKS, LANES), jnp.float32),
                            pltpu.SemaphoreType.DMA((2,))],
        ),
        out_shape=jax.ShapeDtypeStruct((t, D_MODEL), jnp.float32),
        compiler_params=pltpu.CompilerParams(
            dimension_semantics=("arbitrary",), vmem_limit_bytes=VMEM_LIMIT_BYTES),
        name="combine",
    )(dest_flat, gates, x1, p2d, y, *weights)


def kernel(x, p, positions, g_mix, w_in, g_q_lat, w_q_b, g_kv_lat, w_kv_b, g_q_head, g_k_nope, g_k_rope, w_pool, pool_scale, w_attn_out, w_pool_out, w_o, g_ffn, w_router, b_router, w_gate_up, b_gate_up, w_down, b_down, g_ple, w_ple_gate, w_ple):
    batch, seq, _ = x.shape
    tokens = batch * seq
    depth = g_mix.shape[0]
    assert seq % TOKEN_TILE == 0 and seq % ATTN_TILE == 0 and tokens % EXPERT_ROWS == 0
    x2d = x.reshape(tokens, D_MODEL)
    pos2d = positions.astype(jnp.float32).reshape(1, tokens)
    for i in range(depth):
        weights = _premix_weights(g_mix[i], w_in[i], g_q_lat[i], w_q_b[i], g_kv_lat[i], w_kv_b[i],
                                  g_q_head[i], g_k_nope[i], g_k_rope[i], w_pool[i], pool_scale[i],
                                  w_pool_out[i])
        q, kt, v, ga, gp = _premix(x2d, pos2d, weights, batch, seq)
        attn = _attention(q, kt, v).reshape(tokens, N_HEADS * V_HEAD)
        x1, dest, gates, counts, xs = _route(x2d, attn, ga, gp, w_attn_out[i], w_o[i], g_ffn[i],
                                             w_router[i], b_router[i])
        table = _expert_blocks(counts[:, 0], tokens)
        y = _experts(table, xs, w_gate_up[i], b_gate_up[i], w_down[i], b_down[i])
        x2d = _combine(dest[:TOP_K].reshape(-1), gates[:TOP_K].T, x1, p[i].reshape(tokens, PLE_DIM), y,
                       g_ple[i], w_ple_gate[i], w_ple[i])
    return x2d.reshape(batch, seq, D_MODEL)
```

```python
import functools
import math

import jax
import jax.numpy as jnp
from jax import lax
from jax.experimental import pallas as pl
from jax.experimental.pallas import tpu as pltpu

D_MODEL = 1024
N_HEADS = 8
QK_NOPE = 64
QK_ROPE = 32
QK_HEAD = QK_NOPE + QK_ROPE
V_HEAD = 64
Q_LORA = 256
KV_LORA = 128
ROPE_THETA = 10000.0
EPS = 1e-6
POOL_WINDOWS = (2, 4, 8, 16)
POOL_GROUPS = 4
POOL_WIDTH = 512
POOL_GROUP_DIM = POOL_WIDTH // POOL_GROUPS
POOL_HALO = 16
N_EXPERTS = 32
TOP_K = 4
D_EXPERT = 1024
SWIGLU_ALPHA = 1.702
SWIGLU_LIMIT = 7.0
PLE_DIM = 256

LANES = 128
HEAD_SLAB = LANES
ROW_CHUNKS = D_MODEL // LANES
VMEM_LIMIT_BYTES = 56 * 1024 * 1024

MXU_DTYPE = jnp.bfloat16
NEG_BIG = -1e30
Q_SCALE = math.log2(math.e) / math.sqrt(QK_HEAD)

TOKEN_TILE = 512
ATTN_TILE = 512
ATTN_HEADS = 4
EXPERT_ROWS = 512
ISSUE_UNROLL = 8
COMBINE_GROUP = 16


def _dot(a, b):
    return jnp.dot(a, b, preferred_element_type=jnp.float32)


def _rms_scale(v, width):
    return lax.rsqrt(jnp.sum(v * v, axis=-1, keepdims=True) * (1.0 / width) + EPS)


_C_QLAT = 0
_C_KVLAT = _C_QLAT + Q_LORA
_C_KR = _C_KVLAT + KV_LORA
_C_KRR = _C_KR + HEAD_SLAB
_C_POOL = _C_KRR + HEAD_SLAB
_C_GA = _C_POOL + POOL_WIDTH
_C_GP = _C_GA + D_MODEL
_C_END = _C_GP + D_MODEL


def _premix_kernel(tiles_per_seq, x_ref, pos_ref, gmix_ref, w1_ref, gql_ref, wqs_ref, wqr_ref,
                   gkvl_ref, wks_ref, wv_ref, gqs_ref, gqr_ref, gks_ref, gkrs_ref, gkrr_ref,
                   invf_ref, wpool_ref, pscale_ref, wpo_ref,
                   q_ref, kt_ref, v_ref, ga_ref, gp_ref, halo_ref, proj_ref):
    @pl.when(pl.program_id(0) == 0)
    def _():
        halo_ref[...] = jnp.zeros_like(halo_ref)

    xv = x_ref[...]
    h = (xv * _rms_scale(xv, D_MODEL) * gmix_ref[...]).astype(MXU_DTYPE)
    proj_ref[...] = _dot(h, w1_ref[...])
    for section in _premix_sections(tiles_per_seq, proj_ref, pos_ref, gql_ref, wqs_ref, wqr_ref, gkvl_ref,
                                    wks_ref, wv_ref, gqs_ref, gqr_ref, gks_ref, gkrs_ref, gkrr_ref, invf_ref,
                                    wpool_ref, pscale_ref, wpo_ref, q_ref, kt_ref, v_ref, ga_ref, gp_ref,
                                    halo_ref):
        section()


def _premix_sections(tiles_per_seq, proj, pos_ref, gql_ref, wqs_ref, wqr_ref, gkvl_ref, wks_ref, wv_ref,
                     gqs_ref, gqr_ref, gks_ref, gkrs_ref, gkrr_ref, invf_ref, wpool_ref, pscale_ref,
                     wpo_ref, q_ref, kt_ref, v_ref, ga_ref, gp_ref, halo_ref):
    tm = proj.shape[0]
    si = pl.program_id(0) % tiles_per_seq
    shared = {}

    def rope_tables():
        if "cos" not in shared:
            ang = invf_ref[...] * pos_ref[...]
            expand = lambda tab: jnp.tile(tab, (LANES // tab.shape[0], 1)).T
            shared["cos"] = expand(jnp.cos(ang))
            shared["sin"] = expand(jnp.sin(ang))
        return shared["cos"], shared["sin"]

    def q_project():
        q_lat = proj[:, _C_QLAT:_C_QLAT + Q_LORA]
        qln = (q_lat * _rms_scale(q_lat, Q_LORA) * gql_ref[...]).astype(MXU_DTYPE)
        shared["qs"] = _dot(qln, wqs_ref[...])
        shared["qr"] = _dot(qln, wqr_ref[...])

    def q_heads(heads):
        def run():
            cos_t, sin_t = rope_tables()
            lane = lax.broadcasted_iota(jnp.int32, (tm, LANES), 1)
            q_cos = gqs_ref[...] * jnp.where(lane < QK_NOPE, 1.0, cos_t)
            q_sin = gqr_ref[...] * sin_t
            for hd in heads:
                s = shared["qs"][:, hd * HEAD_SLAB:(hd + 1) * HEAD_SLAB]
                r = shared["qr"][:, hd * HEAD_SLAB:(hd + 1) * HEAD_SLAB]
                scale = _rms_scale(s, QK_HEAD) * Q_SCALE
                q_ref[0, hd] = ((s * q_cos + r * q_sin) * scale).astype(q_ref.dtype)
        return run

    def kv_project():
        cos_t, sin_t = rope_tables()
        kv_lat = proj[:, _C_KVLAT:_C_KVLAT + KV_LORA]
        kvn = (kv_lat * _rms_scale(kv_lat, KV_LORA) * gkvl_ref[...]).astype(MXU_DTYPE)
        shared["ks"] = _dot(kvn, wks_ref[...])
        v_ref[0] = _dot(kvn, wv_ref[...]).astype(v_ref.dtype)
        krs = proj[:, _C_KR:_C_KR + HEAD_SLAB]
        krr = proj[:, _C_KRR:_C_KRR + HEAD_SLAB]
        shared["k_rot"] = ((krs * (gkrs_ref[...] * cos_t) + krr * (gkrr_ref[...] * sin_t))
                           * _rms_scale(krs, QK_ROPE))

    def k_heads(heads):
        def run():
            for hd in heads:
                s = shared["ks"][:, hd * HEAD_SLAB:(hd + 1) * HEAD_SLAB]
                kh = s * _rms_scale(s, QK_NOPE) * gks_ref[...] + shared["k_rot"]
                kt_ref[0, hd] = kh.T.astype(kt_ref.dtype)
        return run

    def pool_groups(groups):
        def run():
            t_seq = si * tm + lax.broadcasted_iota(jnp.int32, (tm, 1), 0)
            for g in groups:
                w = POOL_WINDOWS[g]
                cols = slice(g * POOL_GROUP_DIM, (g + 1) * POOL_GROUP_DIM)
                u = proj[:, _C_POOL + cols.start:_C_POOL + cols.stop]
                history = jnp.where(si == 0, 0.0, halo_ref[:, cols])
                acc = jnp.concatenate([history, u], axis=0)
                halo_ref[:, cols] = u[tm - POOL_HALO:, :]
                shift = 1
                while shift < w:
                    acc = acc + pltpu.roll(acc, shift, axis=0)
                    shift *= 2
                count = jnp.minimum(t_seq + 1, w).astype(jnp.float32)
                pooled = acc[POOL_HALO:, :] / count - u
                shared[("mixed", g)] = _dot(pooled.astype(MXU_DTYPE), wpool_ref[g])
        return run

    def pool_out():
        mixed = jnp.concatenate([shared[("mixed", g)] for g in range(POOL_GROUPS)], axis=1) * pscale_ref[...]
        shared["y_pool"] = _dot(mixed.astype(MXU_DTYPE), wpo_ref[...])

    def gate_attn():
        ga_ref[...] = jax.nn.sigmoid(proj[:, _C_GA:_C_GA + D_MODEL]).astype(ga_ref.dtype)

    def gate_pool():
        gp_ref[...] = (jax.nn.sigmoid(proj[:, _C_GP:_C_GP + D_MODEL]) * shared["y_pool"]).astype(gp_ref.dtype)

    half = N_HEADS // 2
    return [q_project, kv_project, q_heads(range(half)), q_heads(range(half, N_HEADS)),
            k_heads(range(half)), k_heads(range(half, N_HEADS)),
            pool_groups((0, 1)), pool_groups((2, 3)), pool_out, gate_attn, gate_pool]


def _premix_weights(g_mix, w_in, g_q_lat, w_q_b, g_kv_lat, w_kv_b, g_q_head, g_k_nope, g_k_rope,
                    w_pool, pool_scale, w_pool_out):
    f32 = jnp.float32
    half = QK_ROPE // 2
    w_q_lat = w_in[:, :Q_LORA]
    w_kv_lat = w_in[:, Q_LORA:Q_LORA + KV_LORA]
    w_kr = w_in[:, Q_LORA + KV_LORA:Q_LORA + KV_LORA + QK_ROPE]
    off = Q_LORA + KV_LORA + QK_ROPE
    w_pool_in = w_in[:, off:off + POOL_WIDTH]
    w_ga = w_in[:, off + POOL_WIDTH:off + POOL_WIDTH + D_MODEL]
    w_gp = w_in[:, off + POOL_WIDTH + D_MODEL:]

    kr_s = jnp.zeros((D_MODEL, HEAD_SLAB), f32).at[:, QK_NOPE:QK_NOPE + QK_ROPE].set(w_kr)
    kr_r = jnp.zeros((D_MODEL, HEAD_SLAB), f32)
    kr_r = kr_r.at[:, QK_NOPE:QK_NOPE + half].set(-w_kr[:, half:])
    kr_r = kr_r.at[:, QK_NOPE + half:QK_NOPE + QK_ROPE].set(w_kr[:, :half])
    w1 = jnp.concatenate([w_q_lat, w_kv_lat, kr_s, kr_r, w_pool_in, w_ga, w_gp], axis=1)

    wq = w_q_b.reshape(Q_LORA, N_HEADS, QK_HEAD)
    wq_s = jnp.zeros((Q_LORA, N_HEADS, HEAD_SLAB), f32).at[:, :, :QK_HEAD].set(wq)
    wq_r = jnp.zeros((Q_LORA, N_HEADS, HEAD_SLAB), f32)
    wq_r = wq_r.at[:, :, QK_NOPE:QK_NOPE + half].set(-wq[:, :, QK_NOPE + half:])
    wq_r = wq_r.at[:, :, QK_NOPE + half:QK_HEAD].set(wq[:, :, QK_NOPE:QK_NOPE + half])
    wkv = w_kv_b.reshape(KV_LORA, N_HEADS, QK_NOPE + V_HEAD)
    wk_s = jnp.zeros((KV_LORA, N_HEADS, HEAD_SLAB), f32).at[:, :, :QK_NOPE].set(wkv[:, :, :QK_NOPE])
    wv = wkv[:, :, QK_NOPE:].reshape(KV_LORA, N_HEADS * V_HEAD)

    def slab(vals, start):
        return jnp.zeros((1, HEAD_SLAB), f32).at[0, start:start + vals.shape[0]].set(vals)

    gq_s = slab(g_q_head, 0)
    gq_r = slab(jnp.concatenate([g_q_head[QK_NOPE + half:], g_q_head[QK_NOPE:QK_NOPE + half]]), QK_NOPE)
    gk_s = slab(g_k_nope, 0)
    gkr_s = slab(g_k_rope, QK_NOPE)
    gkr_r = slab(jnp.concatenate([g_k_rope[half:], g_k_rope[:half]]), QK_NOPE)
    inv_freq = ROPE_THETA ** (-jnp.arange(0, QK_ROPE, 2, dtype=f32) / QK_ROPE)
    invf = inv_freq[:, None]
    bf = MXU_DTYPE
    return (g_mix[None, :], w1.astype(bf), g_q_lat[None, :],
            wq_s.reshape(Q_LORA, -1).astype(bf), wq_r.reshape(Q_LORA, -1).astype(bf),
            g_kv_lat[None, :], wk_s.reshape(KV_LORA, -1).astype(bf), wv.astype(bf),
            gq_s, gq_r, gk_s, gkr_s, gkr_r, invf, w_pool.astype(bf), pool_scale[None, :],
            w_pool_out.astype(bf))


def _full_spec(arr):
    nd = arr.ndim
    return pl.BlockSpec(arr.shape, lambda i, _nd=nd: (0,) * _nd)


def _premix(x2d, pos2d, weights, batch, seq):
    t = x2d.shape[0]
    tm = TOKEN_TILE
    tiles_per_seq = seq // tm
    in_specs = [pl.BlockSpec((tm, D_MODEL), lambda i: (i, 0)),
                pl.BlockSpec((1, tm), lambda i: (0, i))] + [_full_spec(w) for w in weights]
    out_shape = (
        jax.ShapeDtypeStruct((batch, N_HEADS, seq, HEAD_SLAB), MXU_DTYPE),
        jax.ShapeDtypeStruct((batch, N_HEADS, HEAD_SLAB, seq), MXU_DTYPE),
        jax.ShapeDtypeStruct((batch, seq, N_HEADS * V_HEAD), MXU_DTYPE),
        jax.ShapeDtypeStruct((t, D_MODEL), MXU_DTYPE),
        jax.ShapeDtypeStruct((t, D_MODEL), MXU_DTYPE),
    )
    out_specs = (
        pl.BlockSpec((1, N_HEADS, tm, HEAD_SLAB), lambda i: (i // tiles_per_seq, 0, i % tiles_per_seq, 0)),
        pl.BlockSpec((1, N_HEADS, HEAD_SLAB, tm), lambda i: (i // tiles_per_seq, 0, 0, i % tiles_per_seq)),
        pl.BlockSpec((1, tm, N_HEADS * V_HEAD), lambda i: (i // tiles_per_seq, i % tiles_per_seq, 0)),
        pl.BlockSpec((tm, D_MODEL), lambda i: (i, 0)),
        pl.BlockSpec((tm, D_MODEL), lambda i: (i, 0)),
    )
    return pl.pallas_call(
        functools.partial(_premix_kernel, tiles_per_seq),
        grid=(t // tm,),
        in_specs=in_specs,
        out_specs=out_specs,
        out_shape=out_shape,
        scratch_shapes=[pltpu.VMEM((POOL_HALO, POOL_WIDTH), jnp.float32),
                        pltpu.VMEM((tm, _C_END), jnp.float32)],
        compiler_params=pltpu.CompilerParams(
            dimension_semantics=("arbitrary",), vmem_limit_bytes=VMEM_LIMIT_BYTES),
        name="premix",
    )(x2d, pos2d, *weights)


def _attention_kernel(q_ref, kt_ref, v_ref, o_ref, vsel_ref, s_ref, m_ref, acc_ref):
    tq = q_ref.shape[2]
    tk = tq
    qi = pl.program_id(2)
    heads = range(q_ref.shape[1])

    ones_lane = (V_HEAD, 0)

    @pl.when(qi == 0)
    def _():
        for hh in heads:
            v = v_ref[0, :, (hh // 2) * HEAD_SLAB:(hh // 2 + 1) * HEAD_SLAB].astype(jnp.float32)
            lane = lax.broadcasted_iota(jnp.int32, v.shape, 1)
            own = (lane < V_HEAD) if hh % 2 == 0 else (lane >= V_HEAD)
            pad = (lane == ones_lane[hh % 2]).astype(jnp.float32)
            vsel_ref[hh] = jnp.where(own, v, pad).astype(vsel_ref.dtype)

    def scores(blk, slot):
        start = pl.multiple_of(blk * tk, tk)
        for hh in heads:
            s_ref[slot, hh] = _dot(q_ref[0, hh], kt_ref[0, hh, :, pl.ds(start, tk)])

    def softmax_pv(blk, slot, masked):
        start = pl.multiple_of(blk * tk, tk)
        for hh in heads:
            if masked:
                row = lax.broadcasted_iota(jnp.int32, (tq, tk), 0)
                col = lax.broadcasted_iota(jnp.int32, (tq, tk), 1)
                s_ref[slot, hh] = jnp.where(row >= col, s_ref[slot, hh], NEG_BIG)
            m_old = m_ref[hh]
            block_max = jnp.max(s_ref[slot, hh], axis=-1, keepdims=True)
            m_new = jnp.maximum(m_old, block_max)
            p = jnp.exp2(s_ref[slot, hh] - jnp.maximum(m_old[:, :1], block_max)).astype(MXU_DTYPE)
            acc_ref[hh] = jnp.exp2(m_old - m_new) * acc_ref[hh] + _dot(p, vsel_ref[hh, pl.ds(start, tk), :])
            m_ref[hh] = m_new

    m_ref[...] = jnp.full(m_ref.shape, NEG_BIG, jnp.float32)
    acc_ref[...] = jnp.zeros(acc_ref.shape, jnp.float32)
    scores(0, 0)

    def two_blocks(jj, carry):
        scores(2 * jj + 1, 1)
        softmax_pv(2 * jj, 0, False)
        scores(2 * jj + 2, 0)
        softmax_pv(2 * jj + 1, 1, False)
        return carry

    lax.fori_loop(0, qi // 2, two_blocks, 0)

    @pl.when(qi % 2 == 0)
    def _():
        softmax_pv(qi, 0, True)

    @pl.when(qi % 2 == 1)
    def _():
        scores(qi, 1)
        softmax_pv(qi - 1, 0, False)
        softmax_pv(qi, 1, True)

    lane = lax.broadcasted_iota(jnp.int32, (tq, HEAD_SLAB), 1)
    for pair in range(len(heads) // 2):
        acc_a = acc_ref[2 * pair]
        acc_b = acc_ref[2 * pair + 1]
        out_a = acc_a / acc_a[:, ones_lane[0]:ones_lane[0] + 1]
        out_b = acc_b / acc_b[:, ones_lane[1]:ones_lane[1] + 1]
        o_ref[0, :, pair * HEAD_SLAB:(pair + 1) * HEAD_SLAB] = jnp.where(lane < V_HEAD, out_a, out_b).astype(o_ref.dtype)


def _attention(q, kt, v):
    batch, _, seq, _ = q.shape
    tq = ATTN_TILE
    hg = ATTN_HEADS
    return pl.pallas_call(
        _attention_kernel,
        grid=(batch, N_HEADS // hg, seq // tq),
        in_specs=[
            pl.BlockSpec((1, hg, tq, HEAD_SLAB), lambda b, g, i: (b, g, i, 0)),
            pl.BlockSpec((1, hg, HEAD_SLAB, seq), lambda b, g, i: (b, g, 0, 0)),
            pl.BlockSpec((1, seq, hg * V_HEAD), lambda b, g, i: (b, 0, g)),
        ],
        out_specs=pl.BlockSpec((1, tq, hg * V_HEAD), lambda b, g, i: (b, i, g)),
        out_shape=jax.ShapeDtypeStruct((batch, seq, N_HEADS * V_HEAD), MXU_DTYPE),
        scratch_shapes=[pltpu.VMEM((hg, seq, HEAD_SLAB), MXU_DTYPE),
                        pltpu.VMEM((2, hg, tq, tq), jnp.float32),
                        pltpu.VMEM((hg, tq, HEAD_SLAB), jnp.float32),
                        pltpu.VMEM((hg, tq, HEAD_SLAB), jnp.float32)],
        compiler_params=pltpu.CompilerParams(
            dimension_semantics=("arbitrary", "arbitrary", "arbitrary"), vmem_limit_bytes=VMEM_LIMIT_BYTES),
        name="attention",
    )(q, kt, v)


def _row_copy_wait(src_ref, dst_ref, sem):
    pltpu.make_async_copy(src_ref, dst_ref, sem).wait()


def _to_token_tiles(dst_ref, rows2d):
    rows = rows2d.shape[0]
    for c in range(ROW_CHUNKS):
        dst_ref[pl.ds(c, rows, stride=ROW_CHUNKS), :] = rows2d[:, c * LANES:(c + 1) * LANES]


def _from_token_tiles(src_ref):
    rows = src_ref.shape[0] // ROW_CHUNKS
    return jnp.concatenate([src_ref[pl.ds(c, rows, stride=ROW_CHUNKS), :] for c in range(ROW_CHUNKS)], axis=1)


def _token_tile(ref, index):
    return ref.at[pl.ds(pl.multiple_of(index * ROW_CHUNKS, ROW_CHUNKS), ROW_CHUNKS)]


SEL_ROWS = 8


def _route_kernel(tokens, x_ref, at_ref, ga_ref, gp_ref, wao_ref, wo_ref, gffn_ref, wrh_ref, wrl_ref, br_ref,
                  x1_ref, dest_ref, gate_ref, cnt_ref, xs_ref,
                  carry_ref, upper_ref, tiles_ref, stage_ref, dest_smem, stage_sem, sems):
    tm = x_ref.shape[0]
    i = pl.program_id(0)
    n = pl.num_programs(0)

    @pl.when(i == 0)
    def _():
        carry_ref[...] = jnp.zeros_like(carry_ref)
        a_i = lax.broadcasted_iota(jnp.int32, (tm, tm), 0)
        b_i = lax.broadcasted_iota(jnp.int32, (tm, tm), 1)
        upper_ref[...] = (a_i < b_i).astype(upper_ref.dtype)

    y_attn = _dot(at_ref[...], wao_ref[...])
    merged = ga_ref[...].astype(jnp.float32) * y_attn + gp_ref[...].astype(jnp.float32)
    x1_ref[...] = x_ref[...] + _dot(merged.astype(MXU_DTYPE), wo_ref[...])
    x1 = x1_ref[...]
    h2 = x1 * _rms_scale(x1, D_MODEL) * gffn_ref[...]
    slot = i % 2
    _to_token_tiles(tiles_ref.at[slot], h2)

    h_hi = h2.astype(MXU_DTYPE)
    h_lo = (h2 - h_hi.astype(jnp.float32)).astype(MXU_DTYPE)
    logits = _dot(h_hi, wrh_ref[...]) + (_dot(h_lo, wrh_ref[...]) + _dot(h_hi, wrl_ref[...])) + br_ref[...]
    work = logits.T[:N_EXPERTS]

    expert = lax.broadcasted_iota(jnp.int32, (N_EXPERTS, tm), 0)
    vals, idxs = [], []
    for _ in range(TOP_K):
        mx = jnp.max(work, axis=0, keepdims=True)
        ix = jnp.min(jnp.where(work == mx, expert, N_EXPERTS), axis=0, keepdims=True)
        vals.append(mx)
        idxs.append(ix)
        work = jnp.where(expert == ix, -jnp.inf, work)
    exps = [jnp.exp(v - vals[0]) for v in vals]
    denom = exps[0] + exps[1] + exps[2] + exps[3]

    onehot = jnp.zeros((N_EXPERTS, tm), jnp.float32)
    for ix in idxs:
        onehot = onehot + (expert == ix).astype(jnp.float32)
    before = carry_ref[...] + _dot(onehot.astype(MXU_DTYPE), upper_ref[...])
    carry_ref[...] = carry_ref[...] + jnp.sum(onehot, axis=1, keepdims=True)
    cnt_ref[...] = carry_ref[...].astype(jnp.int32)

    sel = lax.broadcasted_iota(jnp.int32, (SEL_ROWS, tm), 0)
    dest = jnp.zeros((SEL_ROWS, tm), jnp.int32)
    gate = jnp.zeros((SEL_ROWS, tm), jnp.float32)
    for k, ix in enumerate(idxs):
        rank = jnp.sum(jnp.where(expert == ix, before, 0.0), axis=0, keepdims=True).astype(jnp.int32)
        dest = jnp.where(sel == k, ix * tokens + rank, dest)
        gate = jnp.where(sel == k, exps[k] / denom, gate)
    dest_ref[...] = dest
    gate_ref[...] = gate

    stage_ref[...] = dest
    to_smem = pltpu.make_async_copy(stage_ref, dest_smem, stage_sem)
    to_smem.start()
    to_smem.wait()

    def issue(t, carry):
        for k in range(TOP_K):
            pltpu.make_async_copy(_token_tile(tiles_ref.at[slot], t), _token_tile(xs_ref, dest_smem[k, t]),
                                  sems.at[slot]).start(priority=k % 2)
        return carry

    lax.fori_loop(0, tm, issue, 0, unroll=ISSUE_UNROLL)

    def wait_tile(s):
        for _ in range(TOP_K):
            _row_copy_wait(tiles_ref.at[s], xs_ref.at[pl.ds(0, tm * ROW_CHUNKS)], sems.at[s])

    @pl.when(i > 0)
    def _():
        wait_tile(1 - slot)

    @pl.when(i == n - 1)
    def _():
        wait_tile(slot)


def _route(x2d, attn2d, ga, gp, w_attn_out, w_o, g_ffn, w_router, b_router):
    t = x2d.shape[0]
    tm = TOKEN_TILE
    f32 = jnp.float32
    wr = jnp.zeros((D_MODEL, LANES), f32).at[:, :N_EXPERTS].set(w_router)
    wr_hi = wr.astype(MXU_DTYPE)
    wr_lo = (wr - wr_hi.astype(f32)).astype(MXU_DTYPE)
    br = jnp.full((1, LANES), NEG_BIG, f32).at[0, :N_EXPERTS].set(b_router)
    weights = (w_attn_out.astype(MXU_DTYPE), w_o.astype(MXU_DTYPE), g_ffn[None, :], wr_hi, wr_lo, br)
    row_spec = lambda width: pl.BlockSpec((tm, width), lambda i: (i, 0))
    sel_spec = pl.BlockSpec((SEL_ROWS, tm), lambda i: (0, i))
    return pl.pallas_call(
        functools.partial(_route_kernel, t),
        grid=(t // tm,),
        in_specs=[row_spec(D_MODEL), row_spec(N_HEADS * V_HEAD), row_spec(D_MODEL), row_spec(D_MODEL)]
        + [_full_spec(w) for w in weights],
        out_specs=(row_spec(D_MODEL), sel_spec, sel_spec,
                   pl.BlockSpec((N_EXPERTS, 1), lambda i: (0, 0)),
                   pl.BlockSpec(memory_space=pl.ANY)),
        out_shape=(
            jax.ShapeDtypeStruct((t, D_MODEL), f32),
            jax.ShapeDtypeStruct((SEL_ROWS, t), jnp.int32),
            jax.ShapeDtypeStruct((SEL_ROWS, t), f32),
            jax.ShapeDtypeStruct((N_EXPERTS, 1), jnp.int32),
            jax.ShapeDtypeStruct((N_EXPERTS * t * ROW_CHUNKS, LANES), f32),
        ),
        scratch_shapes=[pltpu.VMEM((N_EXPERTS, 1), f32),
                        pltpu.VMEM((tm, tm), MXU_DTYPE),
                        pltpu.VMEM((2, tm * ROW_CHUNKS, LANES), f32),
                        pltpu.VMEM((SEL_ROWS, tm), jnp.int32),
                        pltpu.SMEM((SEL_ROWS, tm), jnp.int32),
                        pltpu.SemaphoreType.DMA(()),
                        pltpu.SemaphoreType.DMA((2,))],
        compiler_params=pltpu.CompilerParams(
            dimension_semantics=("arbitrary",), vmem_limit_bytes=VMEM_LIMIT_BYTES, has_side_effects=True),
        name="route",
    )(x2d, attn2d, ga, gp, *weights)


def _experts_kernel(blk_ref, bexp_ref, nval_ref, first_ref, next_ref, slot_ref,
                    xs_ref, wgu_ref, bgu_ref, wd_ref, bd_ref,
                    y_ref, wgu_f32, wd_f32, wgu_bf, wd_bf, gu_ref, sems):
    i = pl.program_id(0)
    rows = xs_ref.shape[0] // ROW_CHUNKS

    def weight_copies(expert, slot):
        return (pltpu.make_async_copy(wgu_ref.at[expert], wgu_f32.at[slot], sems.at[slot, 0]),
                pltpu.make_async_copy(wd_ref.at[expert], wd_f32.at[slot], sems.at[slot, 1]))

    @pl.when(i == 0)
    def _():
        for copy in weight_copies(bexp_ref[0], 0):
            copy.start()

    @pl.when(first_ref[i] == 1)
    def _():
        slot = slot_ref[i]

        @pl.when(next_ref[i] >= 0)
        def _():
            for copy in weight_copies(next_ref[i], 1 - slot):
                copy.start()

        for copy in weight_copies(bexp_ref[i], slot):
            copy.wait()
        wgu_bf[...] = wgu_f32[slot].astype(MXU_DTYPE)
        wd_bf[...] = wd_f32[slot].astype(MXU_DTYPE)

    def ffn(n_rows):
        tiles = pl.ds(0, n_rows * ROW_CHUNKS)
        row = lax.broadcasted_iota(jnp.int32, (n_rows, 1), 0)
        x = jnp.where(row < nval_ref[i], _from_token_tiles(xs_ref.at[tiles]), 0.0).astype(MXU_DTYPE)
        gu = gu_ref.at[pl.ds(0, n_rows)]
        gu[...] = _dot(x, wgu_bf[...]) + bgu_ref[0]
        gate = jnp.minimum(gu[:, :D_EXPERT], SWIGLU_LIMIT)
        up = jnp.clip(gu[:, D_EXPERT:], -SWIGLU_LIMIT, SWIGLU_LIMIT)
        act = (up + 1.0) * gate * jax.nn.sigmoid(SWIGLU_ALPHA * gate)
        _to_token_tiles(y_ref.at[tiles], _dot(act.astype(MXU_DTYPE), wd_bf[...]) + bd_ref[0])

    @pl.when(nval_ref[i] > rows // 2)
    def _():
        ffn(rows)

    @pl.when((nval_ref[i] > 0) & (nval_ref[i] <= rows // 2))
    def _():
        ffn(rows // 2)


def _experts(table, xs, w_gate_up, b_gate_up, w_down, b_down):
    rows = EXPERT_ROWS
    nb = table[0].shape[0]
    row_block = pl.BlockSpec((rows * ROW_CHUNKS, LANES), lambda i, blk, *_: (blk[i], 0))
    return pl.pallas_call(
        _experts_kernel,
        grid_spec=pltpu.PrefetchScalarGridSpec(
            num_scalar_prefetch=len(table),
            grid=(nb,),
            in_specs=[
                row_block,
                pl.BlockSpec(memory_space=pl.ANY),
                pl.BlockSpec((1, 1, 2 * D_EXPERT), lambda i, blk, bexp, *_: (bexp[i], 0, 0)),
                pl.BlockSpec(memory_space=pl.ANY),
                pl.BlockSpec((1, 1, D_MODEL), lambda i, blk, bexp, *_: (bexp[i], 0, 0)),
            ],
            out_specs=row_block,
            scratch_shapes=[pltpu.VMEM((2, D_MODEL, 2 * D_EXPERT), jnp.float32),
                            pltpu.VMEM((2, D_EXPERT, D_MODEL), jnp.float32),
                            pltpu.VMEM((D_MODEL, 2 * D_EXPERT), MXU_DTYPE),
                            pltpu.VMEM((D_EXPERT, D_MODEL), MXU_DTYPE),
                            pltpu.VMEM((rows, 2 * D_EXPERT), jnp.float32),
                            pltpu.SemaphoreType.DMA((2, 2))],
        ),
        out_shape=jax.ShapeDtypeStruct(xs.shape, jnp.float32),
        compiler_params=pltpu.CompilerParams(
            dimension_semantics=("arbitrary",), vmem_limit_bytes=VMEM_LIMIT_BYTES),
        name="experts",
    )(*table, xs, w_gate_up, b_gate_up[:, None, :], w_down, b_down[:, None, :])


def _expert_blocks(counts, tokens):
    rows = EXPERT_ROWS
    nb_max = (tokens * TOP_K) // rows + N_EXPERTS
    nblk = (counts + rows - 1) // rows
    ends = jnp.cumsum(nblk)
    starts = ends - nblk
    total = ends[-1]
    step = jnp.arange(nb_max, dtype=jnp.int32)
    i = jnp.minimum(step, total - 1)[:, None]
    owner = ((i >= starts[None, :]) & (i < ends[None, :])).astype(jnp.int32)
    pick = lambda v: jnp.sum(owner * v[None, :], axis=1)
    e = pick(jnp.arange(N_EXPERTS, dtype=jnp.int32))
    j = i[:, 0] - pick(starts)
    active = step < total
    blk = e * (tokens // rows) + j
    nval = jnp.where(active, jnp.clip(pick(counts) - j * rows, 0, rows), 0)
    first = jnp.where(active & (j == 0), 1, 0)
    ids = jnp.arange(N_EXPERTS, dtype=jnp.int32)
    has_rows = nblk > 0
    later = has_rows[None, :] & (ids[None, :] > ids[:, None])
    nxt = jnp.min(jnp.where(later, ids[None, :], N_EXPERTS), axis=1)
    nxt = jnp.where(nxt == N_EXPERTS, -1, nxt)
    slot = (jnp.cumsum(has_rows.astype(jnp.int32)) - 1) % 2
    table = (blk, e, nval, first, pick(nxt), pick(slot))
    return tuple(col.astype(jnp.int32) for col in table)


def _combine_kernel(tokens, dest_ref, gate_ref, x1_ref, p_ref, y_ref, gple_ref, wpg_ref, wple_ref,
                    o_ref, buf, sems):
    tm = x1_ref.shape[0]
    group = COMBINE_GROUP
    i = pl.program_id(0)
    n = pl.num_programs(0)
    slot = i % 2

    def issue(tile, to_slot, t0):
        for t in range(group):
            for k in range(TOP_K):
                d = dest_ref[k * tokens + tile * tm + t0 + t]
                pltpu.make_async_copy(_token_tile(y_ref, d), _token_tile(buf.at[to_slot, k], t0 + t),
                                      sems.at[to_slot]).start(priority=k % 2)

    def weighted_sum(t0):
        rows = pl.ds(t0, group)
        gates = gate_ref[rows, :]
        for c in range(ROW_CHUNKS):
            cols = slice(c * LANES, (c + 1) * LANES)
            acc = x1_ref[rows, cols]
            for k in range(TOP_K):
                acc = acc + gates[:, k:k + 1] * buf[slot, k, pl.ds(t0 * ROW_CHUNKS + c, group, stride=ROW_CHUNKS), :]
            o_ref[rows, cols] = acc

    @pl.when(i == 0)
    def _():
        lax.fori_loop(0, tm // group, lambda g, c: (issue(0, 0, g * group), c)[1], 0)

    for k in range(TOP_K):
        _row_copy_wait(y_ref.at[pl.ds(0, tm * ROW_CHUNKS)], buf.at[slot, k], sems.at[slot])

    @pl.when(i + 1 < n)
    def _():
        def trip(g, carry):
            t0 = pl.multiple_of(g * group, group)
            issue(i + 1, 1 - slot, t0)
            weighted_sum(t0)
            return carry

        lax.fori_loop(0, tm // group, trip, 0)

    @pl.when(i + 1 == n)
    def _():
        lax.fori_loop(0, tm // group, lambda g, c: (weighted_sum(pl.multiple_of(g * group, group)), c)[1], 0)

    x2 = o_ref[...]
    hp = (x2 * _rms_scale(x2, D_MODEL) * gple_ref[...]).astype(MXU_DTYPE)
    ple_gate = jax.nn.sigmoid(_dot(hp, wpg_ref[...]))
    o_ref[...] = x2 + ple_gate * _dot(p_ref[...].astype(MXU_DTYPE), wple_ref[...])


def _combine(dest_flat, gates, x1, p2d, y, g_ple, w_ple_gate, w_ple):
    t = x1.shape[0]
    tm = TOKEN_TILE
    weights = (g_ple[None, :], w_ple_gate.astype(MXU_DTYPE), w_ple.astype(MXU_DTYPE))
    return pl.pallas_call(
        functools.partial(_combine_kernel, t),
        grid_spec=pltpu.PrefetchScalarGridSpec(
            num_scalar_prefetch=1,
            grid=(t // tm,),
            in_specs=[
                pl.BlockSpec((tm, TOP_K), lambda i, dest: (i, 0)),
                pl.BlockSpec((tm, D_MODEL), lambda i, dest: (i, 0)),
                pl.BlockSpec((tm, PLE_DIM), lambda i, dest: (i, 0)),
                pl.BlockSpec(memory_space=pl.ANY),
            ] + [pl.BlockSpec(w.shape, lambda i, dest, _nd=w.ndim: (0,) * _nd) for w in weights],
            out_specs=pl.BlockSpec((tm, D_MODEL), lambda i, dest: (i, 0)),
            scratch_shapes=[pltpu.VMEM((2, TOP_K, tm * ROW_CHUNKS, LANES), jnp.float32),
                            pltpu.SemaphoreType.DMA((2,))],
        ),
        out_shape=jax.ShapeDtypeStruct((t, D_MODEL), jnp.float32),
        compiler_params=pltpu.CompilerParams(
            dimension_semantics=("arbitrary",), vmem_limit_bytes=VMEM_LIMIT_BYTES),
        name="combine",
    )(dest_flat, gates, x1, p2d, y, *weights)


def kernel(x, p, positions, g_mix, w_in, g_q_lat, w_q_b, g_kv_lat, w_kv_b, g_q_head, g_k_nope, g_k_rope, w_pool, pool_scale, w_attn_out, w_pool_out, w_o, g_ffn, w_router, b_router, w_gate_up, b_gate_up, w_down, b_down, g_ple, w_ple_gate, w_ple):
    batch, seq, _ = x.shape
    tokens = batch * seq
    depth = g_mix.shape[0]
    assert seq % TOKEN_TILE == 0 and seq % ATTN_TILE == 0 and tokens % EXPERT_ROWS == 0
    x2d = x.reshape(tokens, D_MODEL)
    pos2d = positions.astype(jnp.float32).reshape(1, tokens)
    for i in range(depth):
        weights = _premix_weights(g_mix[i], w_in[i], g_q_lat[i], w_q_b[i], g_kv_lat[i], w_kv_b[i],
                                  g_q_head[i], g_k_nope[i], g_k_rope[i], w_pool[i], pool_scale[i],
                                  w_pool_out[i])
        q, kt, v, ga, gp = _premix(x2d, pos2d, weights, batch, seq)
        attn = _attention(q, kt, v).reshape(tokens, N_HEADS * V_HEAD)
        x1, dest, gates, counts, xs = _route(x2d, attn, ga, gp, w_attn_out[i], w_o[i], g_ffn[i],
                                             w_router[i], b_router[i])
        table = _expert_blocks(counts[:, 0], tokens)
        y = _experts(table, xs, w_gate_up[i], b_gate_up[i], w_down[i], b_down[i])
        x2d = _combine(dest[:TOP_K].reshape(-1), gates[:TOP_K].T, x1, p[i].reshape(tokens, PLE_DIM), y,
                       g_ple[i], w_ple_gate[i], w_ple[i])
    return x2d.reshape(batch, seq, D_MODEL)
```

```python
import functools
import math

import jax
import jax.numpy as jnp
from jax import lax
from jax.experimental import pallas as pl
from jax.experimental.pallas import tpu as pltpu

D_MODEL = 1024
N_HEADS = 8
QK_NOPE = 64
QK_ROPE = 32
QK_HEAD = QK_NOPE + QK_ROPE
V_HEAD = 64
Q_LORA = 256
KV_LORA = 128
ROPE_THETA = 10000.0
EPS = 1e-6
POOL_WINDOWS = (2, 4, 8, 16)
POOL_GROUPS = 4
POOL_WIDTH = 512
POOL_GROUP_DIM = POOL_WIDTH // POOL_GROUPS
POOL_HALO = 16
N_EXPERTS = 32
TOP_K = 4
D_EXPERT = 1024
SWIGLU_ALPHA = 1.702
SWIGLU_LIMIT = 7.0
PLE_DIM = 256

LANES = 128
HEAD_SLAB = LANES
ROW_CHUNKS = D_MODEL // LANES
VMEM_LIMIT_BYTES = 56 * 1024 * 1024

MXU_DTYPE = jnp.bfloat16
NEG_BIG = -1e30
Q_SCALE = math.log2(math.e) / math.sqrt(QK_HEAD)

TOKEN_TILE = 512
ATTN_TILE = 512
ATTN_HEADS = 4
EXPERT_ROWS = 512
ISSUE_UNROLL = 8
COMBINE_GROUP = 16


def _dot(a, b):
    return jnp.dot(a, b, preferred_element_type=jnp.float32)


def _rms_scale(v, width):
    return lax.rsqrt(jnp.sum(v * v, axis=-1, keepdims=True) * (1.0 / width) + EPS)


_C_QLAT = 0
_C_KVLAT = _C_QLAT + Q_LORA
_C_KR = _C_KVLAT + KV_LORA
_C_KRR = _C_KR + HEAD_SLAB
_C_POOL = _C_KRR + HEAD_SLAB
_C_GA = _C_POOL + POOL_WIDTH
_C_GP = _C_GA + D_MODEL
_C_END = _C_GP + D_MODEL


def _premix_kernel(tiles_per_seq, x_ref, pos_ref, gmix_ref, w1_ref, gql_ref, wqs_ref, wqr_ref,
                   gkvl_ref, wks_ref, wv_ref, gqs_ref, gqr_ref, gks_ref, gkrs_ref, gkrr_ref,
                   invf_ref, wpool_ref, pscale_ref, wpo_ref,
                   q_ref, kt_ref, v_ref, ga_ref, gp_ref, halo_ref, proj_ref):
    @pl.when(pl.program_id(0) == 0)
    def _():
        halo_ref[...] = jnp.zeros_like(halo_ref)

    xv = x_ref[...]
    h = (xv * _rms_scale(xv, D_MODEL) * gmix_ref[...]).astype(MXU_DTYPE)
    proj_ref[...] = _dot(h, w1_ref[...])
    for section in _premix_sections(tiles_per_seq, proj_ref, pos_ref, gql_ref, wqs_ref, wqr_ref, gkvl_ref,
                                    wks_ref, wv_ref, gqs_ref, gqr_ref, gks_ref, gkrs_ref, gkrr_ref, invf_ref,
                                    wpool_ref, pscale_ref, wpo_ref, q_ref, kt_ref, v_ref, ga_ref, gp_ref,
                                    halo_ref):
        section()


def _premix_sections(tiles_per_seq, proj, pos_ref, gql_ref, wqs_ref, wqr_ref, gkvl_ref, wks_ref, wv_ref,
                     gqs_ref, gqr_ref, gks_ref, gkrs_ref, gkrr_ref, invf_ref, wpool_ref, pscale_ref,
                     wpo_ref, q_ref, kt_ref, v_ref, ga_ref, gp_ref, halo_ref):
    tm = proj.shape[0]
    si = pl.program_id(0) % tiles_per_seq
    shared = {}

    def rope_tables():
        if "cos" not in shared:
            ang = invf_ref[...] * pos_ref[...]
            expand = lambda tab: jnp.tile(tab, (LANES // tab.shape[0], 1)).T
            shared["cos"] = expand(jnp.cos(ang))
            shared["sin"] = expand(jnp.sin(ang))
        return shared["cos"], shared["sin"]

    def q_project():
        q_lat = proj[:, _C_QLAT:_C_QLAT + Q_LORA]
        qln = (q_lat * _rms_scale(q_lat, Q_LORA) * gql_ref[...]).astype(MXU_DTYPE)
        shared["qs"] = _dot(qln, wqs_ref[...])
        shared["qr"] = _dot(qln, wqr_ref[...])

    def q_heads(heads):
        def run():
            cos_t, sin_t = rope_tables()
            lane = lax.broadcasted_iota(jnp.int32, (tm, LANES), 1)
            q_cos = gqs_ref[...] * jnp.where(lane < QK_NOPE, 1.0, cos_t)
            q_sin = gqr_ref[...] * sin_t
            for hd in heads:
                s = shared["qs"][:, hd * HEAD_SLAB:(hd + 1) * HEAD_SLAB]
                r = shared["qr"][:, hd * HEAD_SLAB:(hd + 1) * HEAD_SLAB]
                scale = _rms_scale(s, QK_HEAD) * Q_SCALE
                q_ref[0, hd] = ((s * q_cos + r * q_sin) * scale).astype(q_ref.dtype)
        return run

    def kv_project():
        cos_t, sin_t = rope_tables()
        kv_lat = proj[:, _C_KVLAT:_C_KVLAT + KV_LORA]
        kvn = (kv_lat * _rms_scale(kv_lat, KV_LORA) * gkvl_ref[...]).astype(MXU_DTYPE)
        shared["ks"] = _dot(kvn, wks_ref[...])
        v_ref[0] = _dot(kvn, wv_ref[...]).astype(v_ref.dtype)
        krs = proj[:, _C_KR:_C_KR + HEAD_SLAB]
        krr = proj[:, _C_KRR:_C_KRR + HEAD_SLAB]
        shared["k_rot"] = ((krs * (gkrs_ref[...] * cos_t) + krr * (gkrr_ref[...] * sin_t))
                           * _rms_scale(krs, QK_ROPE))

    def k_heads(heads):
        def run():
            for hd in heads:
                s = shared["ks"][:, hd * HEAD_SLAB:(hd + 1) * HEAD_SLAB]
                kh = s * _rms_scale(s, QK_NOPE) * gks_ref[...] + shared["k_rot"]
                kt_ref[0, hd] = kh.T.astype(kt_ref.dtype)
        return run

    def pool_groups(groups):
        def run():
            t_seq = si * tm + lax.broadcasted_iota(jnp.int32, (tm, 1), 0)
            for g in groups:
                w = POOL_WINDOWS[g]
                cols = slice(g * POOL_GROUP_DIM, (g + 1) * POOL_GROUP_DIM)
                u = proj[:, _C_POOL + cols.start:_C_POOL + cols.stop]
                history = jnp.where(si == 0, 0.0, halo_ref[:, cols])
                acc = jnp.concatenate([history, u], axis=0)
                halo_ref[:, cols] = u[tm - POOL_HALO:, :]
                shift = 1
                while shift < w:
                    acc = acc + pltpu.roll(acc, shift, axis=0)
                    shift *= 2
                count = jnp.minimum(t_seq + 1, w).astype(jnp.float32)
                pooled = acc[POOL_HALO:, :] / count - u
                shared[("mixed", g)] = _dot(pooled.astype(MXU_DTYPE), wpool_ref[g])
        return run

    def pool_out():
        mixed = jnp.concatenate([shared[("mixed", g)] for g in range(POOL_GROUPS)], axis=1) * pscale_ref[...]
        shared["y_pool"] = _dot(mixed.astype(MXU_DTYPE), wpo_ref[...])

    def gate_attn():
        ga_ref[...] = jax.nn.sigmoid(proj[:, _C_GA:_C_GA + D_MODEL]).astype(ga_ref.dtype)

    def gate_pool():
        gp_ref[...] = (jax.nn.sigmoid(proj[:, _C_GP:_C_GP + D_MODEL]) * shared["y_pool"]).astype(gp_ref.dtype)

    half = N_HEADS // 2
    return [q_project, kv_project, q_heads(range(half)), q_heads(range(half, N_HEADS)),
            k_heads(range(half)), k_heads(range(half, N_HEADS)),
            pool_groups((0, 1)), pool_groups((2, 3)), pool_out, gate_attn, gate_pool]


def _premix_weights(g_mix, w_in, g_q_lat, w_q_b, g_kv_lat, w_kv_b, g_q_head, g_k_nope, g_k_rope,
                    w_pool, pool_scale, w_pool_out):
    f32 = jnp.float32
    half = QK_ROPE // 2
    w_q_lat = w_in[:, :Q_LORA]
    w_kv_lat = w_in[:, Q_LORA:Q_LORA + KV_LORA]
    w_kr = w_in[:, Q_LORA + KV_LORA:Q_LORA + KV_LORA + QK_ROPE]
    off = Q_LORA + KV_LORA + QK_ROPE
    w_pool_in = w_in[:, off:off + POOL_WIDTH]
    w_ga = w_in[:, off + POOL_WIDTH:off + POOL_WIDTH + D_MODEL]
    w_gp = w_in[:, off + POOL_WIDTH + D_MODEL:]

    kr_s = jnp.zeros((D_MODEL, HEAD_SLAB), f32).at[:, QK_NOPE:QK_NOPE + QK_ROPE].set(w_kr)
    kr_r = jnp.zeros((D_MODEL, HEAD_SLAB), f32)
    kr_r = kr_r.at[:, QK_NOPE:QK_NOPE + half].set(-w_kr[:, half:])
    kr_r = kr_r.at[:, QK_NOPE + half:QK_NOPE + QK_ROPE].set(w_kr[:, :half])
    w1 = jnp.concatenate([w_q_lat, w_kv_lat, kr_s, kr_r, w_pool_in, w_ga, w_gp], axis=1)

    wq = w_q_b.reshape(Q_LORA, N_HEADS, QK_HEAD)
    wq_s = jnp.zeros((Q_LORA, N_HEADS, HEAD_SLAB), f32).at[:, :, :QK_HEAD].set(wq)
    wq_r = jnp.zeros((Q_LORA, N_HEADS, HEAD_SLAB), f32)
    wq_r = wq_r.at[:, :, QK_NOPE:QK_NOPE + half].set(-wq[:, :, QK_NOPE + half:])
    wq_r = wq_r.at[:, :, QK_NOPE + half:QK_HEAD].set(wq[:, :, QK_NOPE:QK_NOPE + half])
    wkv = w_kv_b.reshape(KV_LORA, N_HEADS, QK_NOPE + V_HEAD)
    wk_s = jnp.zeros((KV_LORA, N_HEADS, HEAD_SLAB), f32).at[:, :, :QK_NOPE].set(wkv[:, :, :QK_NOPE])
    wv = wkv[:, :, QK_NOPE:].reshape(KV_LORA, N_HEADS * V_HEAD)

    def slab(vals, start):
        return jnp.zeros((1, HEAD_SLAB), f32).at[0, start:start + vals.shape[0]].set(vals)

    gq_s = slab(g_q_head, 0)
    gq_r = slab(jnp.concatenate([g_q_head[QK_NOPE + half:], g_q_head[QK_NOPE:QK_NOPE + half]]), QK_NOPE)
    gk_s = slab(g_k_nope, 0)
    gkr_s = slab(g_k_rope, QK_NOPE)
    gkr_r = slab(jnp.concatenate([g_k_rope[half:], g_k_rope[:half]]), QK_NOPE)
    inv_freq = ROPE_THETA ** (-jnp.arange(0, QK_ROPE, 2, dtype=f32) / QK_ROPE)
    invf = inv_freq[:, None]
    bf = MXU_DTYPE
    return (g_mix[None, :], w1.astype(bf), g_q_lat[None, :],
            wq_s.reshape(Q_LORA, -1).astype(bf), wq_r.reshape(Q_LORA, -1).astype(bf),
            g_kv_lat[None, :], wk_s.reshape(KV_LORA, -1).astype(bf), wv.astype(bf),
            gq_s, gq_r, gk_s, gkr_s, gkr_r, invf, w_pool.astype(bf), pool_scale[None, :],
            w_pool_out.astype(bf))


def _full_spec(arr):
    nd = arr.ndim
    return pl.BlockSpec(arr.shape, lambda i, _nd=nd: (0,) * _nd)


def _premix(x2d, pos2d, weights, batch, seq):
    t = x2d.shape[0]
    tm = TOKEN_TILE
    tiles_per_seq = seq // tm
    in_specs = [pl.BlockSpec((tm, D_MODEL), lambda i: (i, 0)),
                pl.BlockSpec((1, tm), lambda i: (0, i))] + [_full_spec(w) for w in weights]
    out_shape = (
        jax.ShapeDtypeStruct((batch, N_HEADS, seq, HEAD_SLAB), MXU_DTYPE),
        jax.ShapeDtypeStruct((batch, N_HEADS, HEAD_SLAB, seq), MXU_DTYPE),
        jax.ShapeDtypeStruct((batch, seq, N_HEADS * V_HEAD), MXU_DTYPE),
        jax.ShapeDtypeStruct((t, D_MODEL), MXU_DTYPE),
        jax.ShapeDtypeStruct((t, D_MODEL), MXU_DTYPE),
    )
    out_specs = (
        pl.BlockSpec((1, N_HEADS, tm, HEAD_SLAB), lambda i: (i // tiles_per_seq, 0, i % tiles_per_seq, 0)),
        pl.BlockSpec((1, N_HEADS, HEAD_SLAB, tm), lambda i: (i // tiles_per_seq, 0, 0, i % tiles_per_seq)),
        pl.BlockSpec((1, tm, N_HEADS * V_HEAD), lambda i: (i // tiles_per_seq, i % tiles_per_seq, 0)),
        pl.BlockSpec((tm, D_MODEL), lambda i: (i, 0)),
        pl.BlockSpec((tm, D_MODEL), lambda i: (i, 0)),
    )
    return pl.pallas_call(
        functools.partial(_premix_kernel, tiles_per_seq),
        grid=(t // tm,),
        in_specs=in_specs,
        out_specs=out_specs,
        out_shape=out_shape,
        scratch_shapes=[pltpu.VMEM((POOL_HALO, POOL_WIDTH), jnp.float32),
                        pltpu.VMEM((tm, _C_END), jnp.float32)],
        compiler_params=pltpu.CompilerParams(
            dimension_semantics=("arbitrary",), vmem_limit_bytes=VMEM_LIMIT_BYTES),
        name="premix",
    )(x2d, pos2d, *weights)


def _attention_kernel(q_ref, kt_ref, v_ref, o_ref, vsel_ref, s_ref, m_ref, acc_ref):
    tq = q_ref.shape[2]
    tk = tq
    qi = pl.program_id(2)
    heads = range(q_ref.shape[1])

    ones_lane = (V_HEAD, 0)

    @pl.when(qi == 0)
    def _():
        for hh in heads:
            v = v_ref[0, :, (hh // 2) * HEAD_SLAB:(hh // 2 + 1) * HEAD_SLAB].astype(jnp.float32)
            lane = lax.broadcasted_iota(jnp.int32, v.shape, 1)
            own = (lane < V_HEAD) if hh % 2 == 0 else (lane >= V_HEAD)
            pad = (lane == ones_lane[hh % 2]).astype(jnp.float32)
            vsel_ref[hh] = jnp.where(own, v, pad).astype(vsel_ref.dtype)

    def scores(blk, slot):
        start = pl.multiple_of(blk * tk, tk)
        for hh in heads:
            s_ref[slot, hh] = _dot(q_ref[0, hh], kt_ref[0, hh, :, pl.ds(start, tk)])

    def softmax_pv(blk, slot, masked):
        start = pl.multiple_of(blk * tk, tk)
        for hh in heads:
            if masked:
                row = lax.broadcasted_iota(jnp.int32, (tq, tk), 0)
                col = lax.broadcasted_iota(jnp.int32, (tq, tk), 1)
                s_ref[slot, hh] = jnp.where(row >= col, s_ref[slot, hh], NEG_BIG)
            m_old = m_ref[hh]
            block_max = jnp.max(s_ref[slot, hh], axis=-1, keepdims=True)
            m_new = jnp.maximum(m_old, block_max)
            p = jnp.exp2(s_ref[slot, hh] - jnp.maximum(m_old[:, :1], block_max)).astype(MXU_DTYPE)
            acc_ref[hh] = jnp.exp2(m_old - m_new) * acc_ref[hh] + _dot(p, vsel_ref[hh, pl.ds(start, tk), :])
            m_ref[hh] = m_new

    m_ref[...] = jnp.full(m_ref.shape, NEG_BIG, jnp.float32)
    acc_ref[...] = jnp.zeros(acc_ref.shape, jnp.float32)
    scores(0, 0)

    def two_blocks(jj, carry):
        scores(2 * jj + 1, 1)
        softmax_pv(2 * jj, 0, False)
        scores(2 * jj + 2, 0)
        softmax_pv(2 * jj + 1, 1, False)
        return carry

    lax.fori_loop(0, qi // 2, two_blocks, 0)

    @pl.when(qi % 2 == 0)
    def _():
        softmax_pv(qi, 0, True)

    @pl.when(qi % 2 == 1)
    def _():
        scores(qi, 1)
        softmax_pv(qi - 1, 0, False)
        softmax_pv(qi, 1, True)

    lane = lax.broadcasted_iota(jnp.int32, (tq, HEAD_SLAB), 1)
    for pair in range(len(heads) // 2):
        acc_a = acc_ref[2 * pair]
        acc_b = acc_ref[2 * pair + 1]
        out_a = acc_a / acc_a[:, ones_lane[0]:ones_lane[0] + 1]
        out_b = acc_b / acc_b[:, ones_lane[1]:ones_lane[1] + 1]
        o_ref[0, :, pair * HEAD_SLAB:(pair + 1) * HEAD_SLAB] = jnp.where(lane < V_HEAD, out_a, out_b).astype(o_ref.dtype)


def _attention(q, kt, v):
    batch, _, seq, _ = q.shape
    tq = ATTN_TILE
    hg = ATTN_HEADS
    return pl.pallas_call(
        _attention_kernel,
        grid=(batch, N_HEADS // hg, seq // tq),
        in_specs=[
            pl.BlockSpec((1, hg, tq, HEAD_SLAB), lambda b, g, i: (b, g, i, 0)),
            pl.BlockSpec((1, hg, HEAD_SLAB, seq), lambda b, g, i: (b, g, 0, 0)),
            pl.BlockSpec((1, seq, hg * V_HEAD), lambda b, g, i: (b, 0, g)),
        ],
        out_specs=pl.BlockSpec((1, tq, hg * V_HEAD), lambda b, g, i: (b, i, g)),
        out_shape=jax.ShapeDtypeStruct((batch, seq, N_HEADS * V_HEAD), MXU_DTYPE),
        scratch_shapes=[pltpu.VMEM((hg, seq, HEAD_SLAB), MXU_DTYPE),
                        pltpu.VMEM((2, hg, tq, tq), jnp.float32),
                        pltpu.VMEM((hg, tq, HEAD_SLAB), jnp.float32),
                        pltpu.VMEM((hg, tq, HEAD_SLAB), jnp.float32)],
        compiler_params=pltpu.CompilerParams(
            dimension_semantics=("arbitrary", "arbitrary", "arbitrary"), vmem_limit_bytes=VMEM_LIMIT_BYTES),
        name="attention",
    )(q, kt, v)


def _row_copy_wait(src_ref, dst_ref, sem):
    pltpu.make_async_copy(src_ref, dst_ref, sem).wait()


def _to_token_tiles(dst_ref, rows2d):
    rows = rows2d.shape[0]
    for c in range(ROW_CHUNKS):
        dst_ref[pl.ds(c, rows, stride=ROW_CHUNKS), :] = rows2d[:, c * LANES:(c + 1) * LANES]


def _from_token_tiles(src_ref):
    rows = src_ref.shape[0] // ROW_CHUNKS
    return jnp.concatenate([src_ref[pl.ds(c, rows, stride=ROW_CHUNKS), :] for c in range(ROW_CHUNKS)], axis=1)


def _token_tile(ref, index):
    return ref.at[pl.ds(pl.multiple_of(index * ROW_CHUNKS, ROW_CHUNKS), ROW_CHUNKS)]


SEL_ROWS = 8


def _route_kernel(tokens, x_ref, at_ref, ga_ref, gp_ref, wao_ref, wo_ref, gffn_ref, wrh_ref, wrl_ref, br_ref,
                  x1_ref, dest_ref, gate_ref, cnt_ref, xs_ref,
                  carry_ref, upper_ref, tiles_ref, stage_ref, dest_smem, stage_sem, sems):
    tm = x_ref.shape[0]
    i = pl.program_id(0)
    n = pl.num_programs(0)

    @pl.when(i == 0)
    def _():
        carry_ref[...] = jnp.zeros_like(carry_ref)
        a_i = lax.broadcasted_iota(jnp.int32, (tm, tm), 0)
        b_i = lax.broadcasted_iota(jnp.int32, (tm, tm), 1)
        upper_ref[...] = (a_i < b_i).astype(upper_ref.dtype)

    x1_ref[...] = _dot(at_ref[...], wao_ref[...])
    merged = ga_ref[...].astype(jnp.float32) * x1_ref[...] + gp_ref[...].astype(jnp.float32)
    x1_ref[...] = x_ref[...] + _dot(merged.astype(MXU_DTYPE), wo_ref[...])
    x1 = x1_ref[...]
    h2 = x1 * _rms_scale(x1, D_MODEL) * gffn_ref[...]
    slot = i % 2
    _to_token_tiles(tiles_ref.at[slot], h2)

    h_hi = h2.astype(MXU_DTYPE)
    h_lo = (h2 - h_hi.astype(jnp.float32)).astype(MXU_DTYPE)
    logits = _dot(h_hi, wrh_ref[...]) + (_dot(h_lo, wrh_ref[...]) + _dot(h_hi, wrl_ref[...])) + br_ref[...]
    work = logits.T[:N_EXPERTS]

    expert = lax.broadcasted_iota(jnp.int32, (N_EXPERTS, tm), 0)
    vals, idxs = [], []
    for _ in range(TOP_K):
        mx = jnp.max(work, axis=0, keepdims=True)
        ix = jnp.min(jnp.where(work == mx, expert, N_EXPERTS), axis=0, keepdims=True)
        vals.append(mx)
        idxs.append(ix)
        work = jnp.where(expert == ix, -jnp.inf, work)
    exps = [jnp.exp(v - vals[0]) for v in vals]
    denom = exps[0] + exps[1] + exps[2] + exps[3]

    onehot = jnp.zeros((N_EXPERTS, tm), jnp.float32)
    for ix in idxs:
        onehot = onehot + (expert == ix).astype(jnp.float32)
    before = carry_ref[...] + _dot(onehot.astype(MXU_DTYPE), upper_ref[...])
    carry_ref[...] = carry_ref[...] + jnp.sum(onehot, axis=1, keepdims=True)
    cnt_ref[...] = carry_ref[...].astype(jnp.int32)

    sel = lax.broadcasted_iota(jnp.int32, (SEL_ROWS, tm), 0)
    dest = jnp.zeros((SEL_ROWS, tm), jnp.int32)
    gate = jnp.zeros((SEL_ROWS, tm), jnp.float32)
    for k, ix in enumerate(idxs):
        rank = jnp.sum(jnp.where(expert == ix, before, 0.0), axis=0, keepdims=True).astype(jnp.int32)
        dest = jnp.where(sel == k, ix * tokens + rank, dest)
        gate = jnp.where(sel == k, exps[k] / denom, gate)
    dest_ref[...] = dest
    gate_ref[...] = gate

    stage_ref[...] = dest
    to_smem = pltpu.make_async_copy(stage_ref, dest_smem, stage_sem)
    to_smem.start()
    to_smem.wait()

    def issue(t, carry):
        for k in range(TOP_K):
            pltpu.make_async_copy(_token_tile(tiles_ref.at[slot], t), _token_tile(xs_ref, dest_smem[k, t]),
                                  sems.at[slot]).start(priority=k % 2)
        return carry

    lax.fori_loop(0, tm, issue, 0, unroll=ISSUE_UNROLL)

    def wait_tile(s):
        for _ in range(TOP_K):
            _row_copy_wait(tiles_ref.at[s], xs_ref.at[pl.ds(0, tm * ROW_CHUNKS)], sems.at[s])

    @pl.when(i > 0)
    def _():
        wait_tile(1 - slot)

    @pl.when(i == n - 1)
    def _():
        wait_tile(slot)


def _route(x2d, attn2d, ga, gp, w_attn_out, w_o, g_ffn, w_router, b_router):
    t = x2d.shape[0]
    tm = TOKEN_TILE
    f32 = jnp.float32
    wr = jnp.zeros((D_MODEL, LANES), f32).at[:, :N_EXPERTS].set(w_router)
    wr_hi = wr.astype(MXU_DTYPE)
    wr_lo = (wr - wr_hi.astype(f32)).astype(MXU_DTYPE)
    br = jnp.full((1, LANES), NEG_BIG, f32).at[0, :N_EXPERTS].set(b_router)
    weights = (w_attn_out.astype(MXU_DTYPE), w_o.astype(MXU_DTYPE), g_ffn[None, :], wr_hi, wr_lo, br)
    row_spec = lambda width: pl.BlockSpec((tm, width), lambda i: (i, 0))
    sel_spec = pl.BlockSpec((SEL_ROWS, tm), lambda i: (0, i))
    return pl.pallas_call(
        functools.partial(_route_kernel, t),
        grid=(t // tm,),
        in_specs=[row_spec(D_MODEL), row_spec(N_HEADS * V_HEAD), row_spec(D_MODEL), row_spec(D_MODEL)]
        + [_full_spec(w) for w in weights],
        out_specs=(row_spec(D_MODEL), sel_spec, sel_spec,
                   pl.BlockSpec((N_EXPERTS, 1), lambda i: (0, 0)),
                   pl.BlockSpec(memory_space=pl.ANY)),
        out_shape=(
            jax.ShapeDtypeStruct((t, D_MODEL), f32),
            jax.ShapeDtypeStruct((SEL_ROWS, t), jnp.int32),
            jax.ShapeDtypeStruct((SEL_ROWS, t), f32),
            jax.ShapeDtypeStruct((N_EXPERTS, 1), jnp.int32),
            jax.ShapeDtypeStruct((N_EXPERTS * t * ROW_CHUNKS, LANES), f32),
        ),
        scratch_shapes=[pltpu.VMEM((N_EXPERTS, 1), f32),
                        pltpu.VMEM((tm, tm), MXU_DTYPE),
                        pltpu.VMEM((2, tm * ROW_CHUNKS, LANES), f32),
                        pltpu.VMEM((SEL_ROWS, tm), jnp.int32),
                        pltpu.SMEM((SEL_ROWS, tm), jnp.int32),
                        pltpu.SemaphoreType.DMA(()),
                        pltpu.SemaphoreType.DMA((2,))],
        compiler_params=pltpu.CompilerParams(
            dimension_semantics=("arbitrary",), vmem_limit_bytes=VMEM_LIMIT_BYTES, has_side_effects=True),
        name="route",
    )(x2d, attn2d, ga, gp, *weights)


def _experts_kernel(blk_ref, bexp_ref, nval_ref, first_ref, next_ref, slot_ref,
                    xs_ref, wgu_ref, bgu_ref, wd_ref, bd_ref,
                    y_ref, wgu_f32, wd_f32, wgu_bf, wd_bf, gu_ref, sems):
    i = pl.program_id(0)
    rows = xs_ref.shape[0] // ROW_CHUNKS

    def weight_copies(expert, slot):
        return (pltpu.make_async_copy(wgu_ref.at[expert], wgu_f32.at[slot], sems.at[slot, 0]),
                pltpu.make_async_copy(wd_ref.at[expert], wd_f32.at[slot], sems.at[slot, 1]))

    @pl.when(i == 0)
    def _():
        for copy in weight_copies(bexp_ref[0], 0):
            copy.start()

    @pl.when(first_ref[i] == 1)
    def _():
        slot = slot_ref[i]

        @pl.when(next_ref[i] >= 0)
        def _():
            for copy in weight_copies(next_ref[i], 1 - slot):
                copy.start()

        for copy in weight_copies(bexp_ref[i], slot):
            copy.wait()
        wgu_bf[...] = wgu_f32[slot].astype(MXU_DTYPE)
        wd_bf[...] = wd_f32[slot].astype(MXU_DTYPE)

    def ffn(n_rows):
        tiles = pl.ds(0, n_rows * ROW_CHUNKS)
        row = lax.broadcasted_iota(jnp.int32, (n_rows, 1), 0)
        x = jnp.where(row < nval_ref[i], _from_token_tiles(xs_ref.at[tiles]), 0.0).astype(MXU_DTYPE)
        gu = gu_ref.at[pl.ds(0, n_rows)]
        gu[...] = _dot(x, wgu_bf[...]) + bgu_ref[0]
        gate = jnp.minimum(gu[:, :D_EXPERT], SWIGLU_LIMIT)
        up = jnp.clip(gu[:, D_EXPERT:], -SWIGLU_LIMIT, SWIGLU_LIMIT)
        act = (up + 1.0) * gate * jax.nn.sigmoid(SWIGLU_ALPHA * gate)
        _to_token_tiles(y_ref.at[tiles], _dot(act.astype(MXU_DTYPE), wd_bf[...]) + bd_ref[0])

    @pl.when(nval_ref[i] > rows // 2)
    def _():
        ffn(rows)

    @pl.when((nval_ref[i] > 0) & (nval_ref[i] <= rows // 2))
    def _():
        ffn(rows // 2)


def _experts(table, xs, w_gate_up, b_gate_up, w_down, b_down):
    rows = EXPERT_ROWS
    nb = table[0].shape[0]
    row_block = pl.BlockSpec((rows * ROW_CHUNKS, LANES), lambda i, blk, *_: (blk[i], 0))
    return pl.pallas_call(
        _experts_kernel,
        grid_spec=pltpu.PrefetchScalarGridSpec(
            num_scalar_prefetch=len(table),
            grid=(nb,),
            in_specs=[
                row_block,
                pl.BlockSpec(memory_space=pl.ANY),
                pl.BlockSpec((1, 1, 2 * D_EXPERT), lambda i, blk, bexp, *_: (bexp[i], 0, 0)),
                pl.BlockSpec(memory_space=pl.ANY),
                pl.BlockSpec((1, 1, D_MODEL), lambda i, blk, bexp, *_: (bexp[i], 0, 0)),
            ],
            out_specs=row_block,
            scratch_shapes=[pltpu.VMEM((2, D_MODEL, 2 * D_EXPERT), jnp.float32),
                            pltpu.VMEM((2, D_EXPERT, D_MODEL), jnp.float32),
                            pltpu.VMEM((D_MODEL, 2 * D_EXPERT), MXU_DTYPE),
                            pltpu.VMEM((D_EXPERT, D_MODEL), MXU_DTYPE),
                            pltpu.VMEM((rows, 2 * D_EXPERT), jnp.float32),
                            pltpu.SemaphoreType.DMA((2, 2))],
        ),
        out_shape=jax.ShapeDtypeStruct(xs.shape, jnp.float32),
        compiler_params=pltpu.CompilerParams(
            dimension_semantics=("arbitrary",), vmem_limit_bytes=VMEM_LIMIT_BYTES),
        name="experts",
    )(*table, xs, w_gate_up, b_gate_up[:, None, :], w_down, b_down[:, None, :])


def _expert_blocks(counts, tokens):
    rows = EXPERT_ROWS
    nb_max = (tokens * TOP_K) // rows + N_EXPERTS
    nblk = (counts + rows - 1) // rows
    ends = jnp.cumsum(nblk)
    starts = ends - nblk
    total = ends[-1]
    step = jnp.arange(nb_max, dtype=jnp.int32)
    i = jnp.minimum(step, total - 1)[:, None]
    owner = ((i >= starts[None, :]) & (i < ends[None, :])).astype(jnp.int32)
    pick = lambda v: jnp.sum(owner * v[None, :], axis=1)
    e = pick(jnp.arange(N_EXPERTS, dtype=jnp.int32))
    j = i[:, 0] - pick(starts)
    active = step < total
    blk = e * (tokens // rows) + j
    nval = jnp.where(active, jnp.clip(pick(counts) - j * rows, 0, rows), 0)
    first = jnp.where(active & (j == 0), 1, 0)
    ids = jnp.arange(N_EXPERTS, dtype=jnp.int32)
    has_rows = nblk > 0
    later = has_rows[None, :] & (ids[None, :] > ids[:, None])
    nxt = jnp.min(jnp.where(later, ids[None, :], N_EXPERTS), axis=1)
    nxt = jnp.where(nxt == N_EXPERTS, -1, nxt)
    slot = (jnp.cumsum(has_rows.astype(jnp.int32)) - 1) % 2
    table = (blk, e, nval, first, pick(nxt), pick(slot))
    return tuple(col.astype(jnp.int32) for col in table)


def _combine_kernel(tokens, dest_ref, gate_ref, x1_ref, p_ref, y_ref, gple_ref, wpg_ref, wple_ref,
                    o_ref, buf, sems):
    tm = x1_ref.shape[0]
    group = COMBINE_GROUP
    i = pl.program_id(0)
    n = pl.num_programs(0)
    slot = i % 2

    def issue(tile, to_slot, t0):
        for t in range(group):
            for k in range(TOP_K):
                d = dest_ref[k * tokens + tile * tm + t0 + t]
                pltpu.make_async_copy(_token_tile(y_ref, d), _token_tile(buf.at[to_slot, k], t0 + t),
                                      sems.at[to_slot]).start(priority=k % 2)

    def weighted_sum(t0):
        rows = pl.ds(t0, group)
        gates = gate_ref[rows, :]
        for c in range(ROW_CHUNKS):
            cols = slice(c * LANES, (c + 1) * LANES)
            acc = x1_ref[rows, cols]
            for k in range(TOP_K):
                acc = acc + gates[:, k:k + 1] * buf[slot, k, pl.ds(t0 * ROW_CHUNKS + c, group, stride=ROW_CHUNKS), :]
            o_ref[rows, cols] = acc

    @pl.when(i == 0)
    def _():
        lax.fori_loop(0, tm // group, lambda g, c: (issue(0, 0, g * group), c)[1], 0)

    for k in range(TOP_K):
        _row_copy_wait(y_ref.at[pl.ds(0, tm * ROW_CHUNKS)], buf.at[slot, k], sems.at[slot])

    @pl.when(i + 1 < n)
    def _():
        def trip(g, carry):
            t0 = pl.multiple_of(g * group, group)
            issue(i + 1, 1 - slot, t0)
            weighted_sum(t0)
            return carry

        lax.fori_loop(0, tm // group, trip, 0)

    @pl.when(i + 1 == n)
    def _():
        lax.fori_loop(0, tm // group, lambda g, c: (weighted_sum(pl.multiple_of(g * group, group)), c)[1], 0)

    x2 = o_ref[...]
    hp = (x2 * _rms_scale(x2, D_MODEL) * gple_ref[...]).astype(MXU_DTYPE)
    ple_gate = jax.nn.sigmoid(_dot(hp, wpg_ref[...]))
    o_ref[...] = x2 + ple_gate * _dot(p_ref[...].astype(MXU_DTYPE), wple_ref[...])


def _combine(dest_flat, gates, x1, p2d, y, g_ple, w_ple_gate, w_ple):
    t = x1.shape[0]
    tm = TOKEN_TILE
    weights = (g_ple[None, :], w_ple_gate.astype(MXU_DTYPE), w_ple.astype(MXU_DTYPE))
    return pl.pallas_call(
        functools.partial(_combine_kernel, t),
        grid_spec=pltpu.PrefetchScalarGridSpec(
            num_scalar_prefetch=1,
            grid=(t // tm,),
            in_specs=[
                pl.BlockSpec((tm, TOP_K), lambda i, dest: (i, 0)),
                pl.BlockSpec((tm, D_MODEL), lambda i, dest: (i, 0)),
                pl.BlockSpec((tm, PLE_DIM), lambda i, dest: (i, 0)),
                pl.BlockSpec(memory_space=pl.ANY),
            ] + [pl.BlockSpec(w.shape, lambda i, dest, _nd=w.ndim: (0,) * _nd) for w in weights],
            out_specs=pl.BlockSpec((tm, D_MODEL), lambda i, dest: (i, 0)),
            scratch_shapes=[pltpu.VMEM((2, TOP_K, tm * ROW_CHUNKS, LANES), jnp.float32),
                            pltpu.SemaphoreType.DMA((2,))],
        ),
        out_shape=jax.ShapeDtypeStruct((t, D_MODEL), jnp.float32),
        compiler_params=pltpu.CompilerParams(
            dimension_semantics=("arbitrary",), vmem_limit_bytes=VMEM_LIMIT_BYTES),
        name="combine",
    )(dest_flat, gates, x1, p2d, y, *weights)


def kernel(x, p, positions, g_mix, w_in, g_q_lat, w_q_b, g_kv_lat, w_kv_b, g_q_head, g_k_nope, g_k_rope, w_pool, pool_scale, w_attn_out, w_pool_out, w_o, g_ffn, w_router, b_router, w_gate_up, b_gate_up, w_down, b_down, g_ple, w_ple_gate, w_ple):
    batch, seq, _ = x.shape
    tokens = batch * seq
    depth = g_mix.shape[0]
    assert seq % TOKEN_TILE == 0 and seq % ATTN_TILE == 0 and tokens % EXPERT_ROWS == 0
    x2d = x.reshape(tokens, D_MODEL)
    pos2d = positions.astype(jnp.float32).reshape(1, tokens)
    for i in range(depth):
        weights = _premix_weights(g_mix[i], w_in[i], g_q_lat[i], w_q_b[i], g_kv_lat[i], w_kv_b[i],
                                  g_q_head[i], g_k_nope[i], g_k_rope[i], w_pool[i], pool_scale[i],
                                  w_pool_out[i])
        q, kt, v, ga, gp = _premix(x2d, pos2d, weights, batch, seq)
        attn = _attention(q, kt, v).reshape(tokens, N_HEADS * V_HEAD)
        x1, dest, gates, counts, xs = _route(x2d, attn, ga, gp, w_attn_out[i], w_o[i], g_ffn[i],
                                             w_router[i], b_router[i])
        table = _expert_blocks(counts[:, 0], tokens)
        y = _experts(table, xs, w_gate_up[i], b_gate_up[i], w_down[i], b_down[i])
        x2d = _combine(dest[:TOP_K].reshape(-1), gates[:TOP_K].T, x1, p[i].reshape(tokens, PLE_DIM), y,
                       g_ple[i], w_ple_gate[i], w_ple[i])
    return x2d.reshape(batch, seq, D_MODEL)
```
